```python
import math
import jax, jax.numpy as jnp
from jax import lax
import numpy as np

D_MODEL = 1024
BATCH = 8
SEQ = 4096
DEPTH = 2

CHUNK = 64
BRANCH_WIDTH = 512
N_BRANCH = 4
CONV_WIDTH = 3
SG_CHUNK = 128
SG_GROUPS = 4
SG_GROUP_DIM = BRANCH_WIDTH // SG_GROUPS
ATT_HEADS = 4
ATT_QK_DIM = 64
ATT_V_DIM = 2 * ATT_QK_DIM
Q_BLOCK = 128
SSM_GROUP = 16
SSM_GROUPS = BRANCH_WIDTH // SSM_GROUP
SSM_STATE = 64
FFN_HIDDEN = -(-8 * D_MODEL // (3 * 256)) * 256

QK_COLS = ATT_HEADS * 2 * ATT_QK_DIM
V_COLS = ATT_HEADS * ATT_V_DIM
SPLIT_SIZES = [BRANCH_WIDTH, BRANCH_WIDTH, BRANCH_WIDTH,
               2 * BRANCH_WIDTH,
               QK_COLS, QK_COLS, V_COLS,
               BRANCH_WIDTH,
               N_BRANCH * D_MODEL]
IN_COLS = sum(SPLIT_SIZES)
SPLIT_POINTS = [int(p) for p in np.cumsum(SPLIT_SIZES)[:-1]]

kernel_name = 'hybrid_gated_parallel_streaming_encoder'


def rmsnorm(x, g, eps=1e-6):
    xf = x.astype(jnp.float32)
    y = xf * lax.rsqrt(jnp.mean(xf * xf, axis=-1, keepdims=True) + eps)
    return (y * g.astype(jnp.float32)).astype(x.dtype)


def layernorm(x, g, b, eps=1e-5):
    xf = x.astype(jnp.float32)
    mu = jnp.mean(xf, axis=-1, keepdims=True)
    var = jnp.mean(jnp.square(xf - mu), axis=-1, keepdims=True)
    y = (xf - mu) * lax.rsqrt(var + eps)
    return (y * g.astype(jnp.float32) + b.astype(jnp.float32)).astype(x.dtype)


def short_conv_mixer(b_gate, c_gate, xin, conv_w, conv_b):
    z = c_gate * xin
    z = lax.conv_general_dilated(
        z, conv_w[:, None, :], window_strides=(1,), padding=[(CONV_WIDTH - 1, 0)],
        dimension_numbers=('NWC', 'WIO', 'NWC'), feature_group_count=BRANCH_WIDTH)
    return b_gate * (z + conv_b)


def spatial_gating_mixer(z, sg_w, sg_b, ln_g, ln_b):
    z = jax.nn.gelu(z)
    u, v = jnp.split(z, 2, axis=-1)
    v = layernorm(v, ln_g, ln_b)
    bsz, s, _ = v.shape
    v = v.reshape(bsz, s // SG_CHUNK, SG_CHUNK, SG_GROUPS, SG_GROUP_DIM)
    tri = jnp.tril(jnp.ones((SG_CHUNK, SG_CHUNK), dtype=bool))
    w = jnp.where(tri, sg_w, jnp.zeros_like(sg_w))
    mixed = jnp.einsum('gts,bcsgd->bctgd', w, v) + sg_b.T[:, :, None]
    return u * mixed.reshape(bsz, s, BRANCH_WIDTH)


def diff_attention_mixer(q, k, v, lam_qk, subln_g, lam_init):
    bsz, s, _ = q.shape
    q = q.reshape(bsz, s, ATT_HEADS, 2, ATT_QK_DIM)
    k = k.reshape(bsz, s, ATT_HEADS, 2, ATT_QK_DIM)
    v = v.reshape(bsz, s, ATT_HEADS, ATT_V_DIM)
    lf = lam_qk.astype(jnp.float32)
    lam = jnp.exp(jnp.sum(lf[0] * lf[1])) - jnp.exp(jnp.sum(lf[2] * lf[3])) + lam_init
    scale = ATT_QK_DIM ** -0.5
    n_blocks = s // Q_BLOCK
    q_blocks = q.reshape(bsz, n_blocks, Q_BLOCK, ATT_HEADS, 2, ATT_QK_DIM).transpose(1, 0, 2, 3, 4, 5)
    key_chunk = jnp.arange(s) // CHUNK

    def attend(args):
        qb, blk = args
        q_chunk = (blk * Q_BLOCK + jnp.arange(Q_BLOCK)) // CHUNK
        allowed = key_chunk[None, :] <= q_chunk[:, None]
        sc = jnp.einsum('bqhmd,bkhmd->bhmqk', qb, k).astype(jnp.float32) * scale
        sc = jnp.where(allowed, sc, -jnp.inf)
        p = jax.nn.softmax(sc, axis=-1)
        attn = p[:, :, 0] - lam * p[:, :, 1]
        return jnp.einsum('bhqk,bkhd->bqhd', attn.astype(v.dtype), v)

    o = lax.map(attend, (q_blocks, jnp.arange(n_blocks)))
    o = o.transpose(1, 0, 2, 3, 4).reshape(bsz, s, ATT_HEADS, ATT_V_DIM)
    o = rmsnorm(o, subln_g, eps=1e-5) * (1.0 - lam_init)
    return o.reshape(bsz, s, BRANCH_WIDTH)


def _linear_recurrence_combine(earlier, later):
    ar1, ai1, br1, bi1 = earlier
    ar2, ai2, br2, bi2 = later
    return (ar2 * ar1 - ai2 * ai1,
            ar2 * ai1 + ai2 * ar1,
            ar2 * br1 - ai2 * bi1 + br2,
            ar2 * bi1 + ai2 * br1 + bi2)


def s5_mixer(u, a_re, a_im, log_dt, b_re, b_im, c_re, c_im, d_skip, w_glu, b_glu):
    dtype = u.dtype
    f32 = jnp.float32
    bsz, s, _ = u.shape
    uf = u.astype(f32).reshape(bsz, s, SSM_GROUPS, SSM_GROUP)
    a_re, a_im = a_re.astype(f32), a_im.astype(f32)
    b_re, b_im = b_re.astype(f32), b_im.astype(f32)
    c_re, c_im = c_re.astype(f32), c_im.astype(f32)
    dt = jnp.exp(log_dt.astype(f32))[:, None]
    mag = jnp.exp(dt * a_re)
    ab_re = mag * jnp.cos(dt * a_im)
    ab_im = mag * jnp.sin(dt * a_im)
    den = a_re * a_re + a_im * a_im
    nr, ni = ab_re - 1.0, ab_im
    coef_re = (nr * a_re + ni * a_im) / den
    coef_im = (ni * a_re - nr * a_im) / den
    bb_re = coef_re[..., None] * b_re - coef_im[..., None] * b_im
    bb_im = coef_re[..., None] * b_im + coef_im[..., None] * b_re
    bu_re = jnp.einsum('bsgh,gph->bsgp', uf, bb_re)
    bu_im = jnp.einsum('bsgh,gph->bsgp', uf, bb_im)
    a_seq_re = jnp.broadcast_to(ab_re, bu_re.shape)
    a_seq_im = jnp.broadcast_to(ab_im, bu_re.shape)
    _, _, x_re, x_im = lax.associative_scan(
        _linear_recurrence_combine, (a_seq_re, a_seq_im, bu_re, bu_im), axis=1)
    y = jnp.einsum('bsgp,ghp->bsgh', x_re, c_re) - jnp.einsum('bsgp,ghp->bsgh', x_im, c_im)
    y = y.reshape(bsz, s, BRANCH_WIDTH) + d_skip.astype(f32) * uf.reshape(bsz, s, BRANCH_WIDTH)
    y = jax.nn.gelu(y)
    y = y * jax.nn.sigmoid(y @ w_glu.astype(f32) + b_glu.astype(f32))
    return y.astype(dtype)


def setup_inputs(seed: int = 0) -> dict:
    key = jax.random.key(seed)
    ks = jax.random.split(key, 32)
    f32 = jnp.float32
    L, D, W = DEPTH, D_MODEL, BRANCH_WIDTH
    G, P, H = SSM_GROUPS, SSM_STATE, SSM_GROUP

    def nrm(k, shape, scale):
        return jax.random.normal(k, shape, f32) * scale

    n_idx = jnp.arange(P, dtype=f32)
    return {
        'x': nrm(ks[0], (BATCH, SEQ, D), 1.0),
        'g_mix': 1.0 + nrm(ks[1], (L, D), 0.02),
        'w_in': nrm(ks[2], (L, D, IN_COLS), D ** -0.5),
        'conv_w': nrm(ks[3], (L, CONV_WIDTH, W), CONV_WIDTH ** -0.5),
        'conv_b': nrm(ks[4], (L, W), 0.02),
        'sg_w': nrm(ks[5], (L, SG_GROUPS, SG_CHUNK, SG_CHUNK), SG_CHUNK ** -0.5),
        'sg_b': 1.0 + nrm(ks[6], (L, SG_GROUPS, SG_CHUNK), 0.02),
        'sg_ln_g': 1.0 + nrm(ks[7], (L, W), 0.02),
        'sg_ln_b': nrm(ks[8], (L, W), 0.02),
        'lam_qk': nrm(ks[9], (L, 4, ATT_QK_DIM), 0.1),
        'subln_g': 1.0 + nrm(ks[10], (L, ATT_V_DIM), 0.02),
        'ssm_a_re': -0.5 + nrm(ks[11], (L, G, P), 0.01),
        'ssm_a_im': jnp.pi * n_idx + nrm(ks[12], (L, G, P), 0.01),
        'ssm_log_dt': jax.random.uniform(ks[13], (L, G), f32, math.log(1e-3), math.log(1e-1)),
        'ssm_b_re': nrm(ks[14], (L, G, P, H), (2 * H) ** -0.5),
        'ssm_b_im': nrm(ks[15], (L, G, P, H), (2 * H) ** -0.5),
        'ssm_c_re': nrm(ks[16], (L, G, H, P), P ** -0.5),
        'ssm_c_im': nrm(ks[17], (L, G, H, P), P ** -0.5),
        'ssm_d': nrm(ks[18], (L, W), 0.5),
        'w_glu': nrm(ks[19], (L, W, W), W ** -0.5),
        'b_glu': nrm(ks[20], (L, W), 0.02),
        'w_br': nrm(ks[21], (L, N_BRANCH, W, D), W ** -0.5),
        'w_o': nrm(ks[22], (L, D, D), D ** -0.5),
        'g_ffn': 1.0 + nrm(ks[23], (L, D), 0.02),
        'w_ffn_gate': nrm(ks[24], (L, D, FFN_HIDDEN), D ** -0.5),
        'w_ffn_up': nrm(ks[25], (L, D, FFN_HIDDEN), D ** -0.5),
        'w_ffn_down': nrm(ks[26], (L, FFN_HIDDEN, D), FFN_HIDDEN ** -0.5),
        'g_final': 1.0 + nrm(ks[27], (D,), 0.02),
    }


def reference(x, g_mix, w_in, conv_w, conv_b, sg_w, sg_b, sg_ln_g, sg_ln_b, lam_qk, subln_g,
              ssm_a_re, ssm_a_im, ssm_log_dt, ssm_b_re, ssm_b_im, ssm_c_re, ssm_c_im, ssm_d,
              w_glu, b_glu, w_br, w_o, g_ffn, w_ffn_gate, w_ffn_up, w_ffn_down, g_final):
    bsz, s, _ = x.shape
    for l in range(DEPTH):
        lam_init = 0.8 - 0.6 * math.exp(-0.3 * l)
        h = rmsnorm(x, g_mix[l])
        proj = h @ w_in[l]
        a_b, a_c, a_x, b_uv, c_q, c_k, c_v, d_u, gate_logits = jnp.split(proj, SPLIT_POINTS, axis=-1)

        y_a = short_conv_mixer(a_b, a_c, a_x, conv_w[l], conv_b[l])
        y_b = spatial_gating_mixer(b_uv, sg_w[l], sg_b[l], sg_ln_g[l], sg_ln_b[l])
        y_c = diff_attention_mixer(c_q, c_k, c_v, lam_qk[l], subln_g[l], lam_init)
        y_d = s5_mixer(d_u, ssm_a_re[l], ssm_a_im[l], ssm_log_dt[l], ssm_b_re[l], ssm_b_im[l],
                       ssm_c_re[l], ssm_c_im[l], ssm_d[l], w_glu[l], b_glu[l])

        gates = jax.nn.sigmoid(gate_logits).reshape(bsz, s, N_BRANCH, D_MODEL)
        branches = (y_a, y_b, y_c, y_d)
        merged = gates[:, :, 0] * (branches[0] @ w_br[l, 0])
        for n in range(1, N_BRANCH):
            merged = merged + gates[:, :, n] * (branches[n] @ w_br[l, n])
        x = x + merged @ w_o[l]

        h = rmsnorm(x, g_ffn[l])
        x = x + (jax.nn.silu(h @ w_ffn_gate[l]) * (h @ w_ffn_up[l])) @ w_ffn_down[l]
    return rmsnorm(x, g_final)
```

```python
import functools
import math

import jax
import jax.numpy as jnp
from jax import lax
from jax.experimental import pallas as pl
from jax.experimental.pallas import tpu as pltpu

F32 = jnp.float32
BF16 = jnp.bfloat16

D_MODEL = 1024
WIDTH = 512
N_BRANCH = 4
CHUNK = 64
SG_CHUNK = 128
SG_GROUPS = 4
ATT_HEADS = 4
ATT_QK_DIM = 64
ATT_V_DIM = 128
SSM_GROUP = 16
SSM_GROUPS = 32
SSM_STATE = 64
SSM_LANES = SSM_GROUPS * SSM_STATE
FFN_HIDDEN = 2816

GATE_COLS = N_BRANCH * D_MODEL
BLK_AB, BLK_AC, BLK_AX, BLK_BU, BLK_BV, BLK_Q, BLK_K, BLK_V, BLK_DU = range(8, 17)
PROJ_BLOCKS = 16
PROJ_COLS = PROJ_BLOCKS * WIDTH

VMEM_LIMIT = 56 * 1024 * 1024


def _rms(x, g, eps):
    return x * lax.rsqrt(jnp.mean(x * x, axis=-1, keepdims=True) + eps) * g


def _inproj_kernel(x_ref, g_ref, w_ref, o_ref, du_ref, h_ref):
    j = pl.program_id(1)

    @pl.when(j == 0)
    def _():
        h_ref[...] = _rms(x_ref[...], g_ref[...], 1e-6).astype(BF16)

    acc = jnp.dot(h_ref[...], w_ref[...], preferred_element_type=F32)

    @pl.when(j < PROJ_BLOCKS)
    def _():
        o_ref[...] = acc.astype(BF16)

    @pl.when(j == BLK_DU)
    def _():
        du_ref[...] = acc.astype(BF16)


def _inproj(x2d, g, w, bsz, seq, tm):
    n = x2d.shape[0]
    spb = seq // tm
    return pl.pallas_call(
        _inproj_kernel,
        grid=(n // tm, PROJ_BLOCKS + 1),
        in_specs=[
            pl.BlockSpec((tm, D_MODEL), lambda i, j: (i, 0)),
            pl.BlockSpec((1, D_MODEL), lambda i, j: (0, 0)),
            pl.BlockSpec((D_MODEL, WIDTH), lambda i, j: (0, j)),
        ],
        out_specs=[
            pl.BlockSpec((tm, WIDTH), lambda i, j: (i, jnp.minimum(j, PROJ_BLOCKS - 1))),
            pl.BlockSpec((tm, WIDTH), lambda i, j: (i % spb, i // spb)),
        ],
        out_shape=[
            jax.ShapeDtypeStruct((n, PROJ_COLS), BF16),
            jax.ShapeDtypeStruct((seq, bsz * WIDTH), BF16),
        ],
        scratch_shapes=[pltpu.VMEM((tm, D_MODEL), BF16)],
        compiler_params=pltpu.CompilerParams(
            dimension_semantics=("parallel", "arbitrary"), vmem_limit_bytes=VMEM_LIMIT),
        name="inproj",
    )(x2d, g, w)


def _attn_kernel(lam_ref, g_ref, q_ref, k_ref, v_ref, o_ref, *, tq, lam_init):
    qi = pl.program_id(2)
    lf = lam_ref[...]
    lam = (jnp.exp(jnp.sum(lf[0:1] * lf[1:2], keepdims=True))
           - jnp.exp(jnp.sum(lf[2:3] * lf[3:4], keepdims=True)) + lam_init)

    q = q_ref[...] * (ATT_QK_DIM ** -0.5)
    lane = lax.broadcasted_iota(jnp.int32, q.shape, 1)
    zero = jnp.zeros_like(q)
    qq = jnp.concatenate([jnp.where(lane < ATT_QK_DIM, q, zero),
                          jnp.where(lane >= ATT_QK_DIM, q, zero)], axis=0)

    def scores(kb):
        start = pl.multiple_of(kb * tq, tq)
        k = k_ref[pl.ds(start, tq), :]
        v = v_ref[pl.ds(start, tq), :]
        s = lax.dot_general(qq, k, (((1,), (1,)), ((), ())), preferred_element_type=F32)
        return s, v

    s, v = scores(qi)
    row = lax.broadcasted_iota(jnp.int32, s.shape, 0)
    col = lax.broadcasted_iota(jnp.int32, s.shape, 1)
    row = jnp.where(row >= tq, row - tq, row)
    s = jnp.where(col // CHUNK <= row // CHUNK, s, -jnp.inf)
    m0 = jnp.max(s, axis=-1, keepdims=True)
    p = jnp.exp(s - m0)
    l0 = jnp.sum(p, axis=-1, keepdims=True)
    acc0 = jnp.dot(p.astype(BF16), v, preferred_element_type=F32)

    def body(kb, carry):
        m, l, acc = carry
        s, v = scores(kb)
        m_new = jnp.maximum(m, jnp.max(s, axis=-1, keepdims=True))
        alpha = jnp.exp(m - m_new)
        p = jnp.exp(s - m_new)
        l = alpha * l + jnp.sum(p, axis=-1, keepdims=True)
        acc = alpha * acc + jnp.dot(p.astype(BF16), v, preferred_element_type=F32)
        return m_new, l, acc

    _, l, acc = lax.fori_loop(0, qi, body, (m0, l0, acc0))
    on = acc / l
    o = on[:tq] - lam * on[tq:]
    o_ref[...] = (_rms(o, g_ref[...], 1e-5) * (1.0 - lam_init)).astype(o_ref.dtype)


def _attention(proj3, lam_qk, subln_g, lam_init, tq):
    bsz, seq, _ = proj3.shape
    lanes = ATT_V_DIM
    per = WIDTH // lanes
    return pl.pallas_call(
        functools.partial(_attn_kernel, tq=tq, lam_init=lam_init),
        grid=(bsz, ATT_HEADS, seq // tq),
        in_specs=[
            pl.BlockSpec((4, ATT_QK_DIM), lambda b, h, i: (0, 0)),
            pl.BlockSpec((1, ATT_V_DIM), lambda b, h, i: (0, 0)),
            pl.BlockSpec((None, tq, lanes), lambda b, h, i: (b, i, BLK_Q * per + h)),
            pl.BlockSpec((None, seq, lanes), lambda b, h, i: (b, 0, BLK_K * per + h)),
            pl.BlockSpec((None, seq, lanes), lambda b, h, i: (b, 0, BLK_V * per + h)),
        ],
        out_specs=pl.BlockSpec((None, tq, lanes), lambda b, h, i: (b, i, h)),
        out_shape=jax.ShapeDtypeStruct((bsz, seq, WIDTH), BF16),
        compiler_params=pltpu.CompilerParams(
            dimension_semantics=("parallel", "parallel", "arbitrary"),
            vmem_limit_bytes=VMEM_LIMIT),
        name="diff_attention",
    )(lam_qk, subln_g, proj3, proj3, proj3)


def _ssm_kernel(u_ref, bbig_ref, ar_ref, ai_ref, cbig_ref, d_ref, wglu_ref, bglu_ref,
                o_ref, st_ref, bu_ref, *, bsz, steps, slab):
    @pl.when(pl.program_id(0) == 0)
    def _():
        st_ref[...] = jnp.zeros_like(st_ref)

    u = u_ref[...]
    bu_ref[...] = jnp.dot(u, bbig_ref[...], preferred_element_type=F32)

    for sl in range(SSM_LANES // slab):
        re = pl.ds(sl * slab, slab)
        im = pl.ds(SSM_LANES + sl * slab, slab)
        ar = jnp.broadcast_to(ar_ref[:, re], (bsz, slab))
        ai = jnp.broadcast_to(ai_ref[:, re], (bsz, slab))

        def body(t, carry, re=re, im=im, ar=ar, ai=ai):
            xr, xi = carry
            rows = pl.ds(pl.multiple_of(t * bsz, bsz), bsz)
            nxr = ar * xr - ai * xi + bu_ref[rows, re]
            nxi = ar * xi + ai * xr + bu_ref[rows, im]
            bu_ref[rows, re] = nxr
            bu_ref[rows, im] = nxi
            return nxr, nxi

        xr, xi = lax.fori_loop(0, steps, body, (st_ref[:, re], st_ref[:, im]), unroll=4)
        st_ref[:, re] = xr
        st_ref[:, im] = xi

    y = jnp.dot(bu_ref[...].astype(BF16), cbig_ref[...], preferred_element_type=F32)
    y = jax.nn.gelu(y + d_ref[...] * u.astype(F32))
    z = jnp.dot(y.astype(BF16), wglu_ref[...], preferred_element_type=F32) + bglu_ref[...]
    o_ref[...] = (y * jax.nn.sigmoid(z)).astype(o_ref.dtype)


def _ssm(du_rows, bbig, ar, ai, cbig, d_skip, w_glu, b_glu, bsz, seq, steps):
    rows = steps * bsz
    full = lambda c: (0, 0)
    return pl.pallas_call(
        functools.partial(_ssm_kernel, bsz=bsz, steps=steps, slab=512),
        grid=(seq // steps,),
        in_specs=[
            pl.BlockSpec((rows, WIDTH), lambda c: (c, 0)),
            pl.BlockSpec((WIDTH, 2 * SSM_LANES), full),
            pl.BlockSpec((1, SSM_LANES), full),
            pl.BlockSpec((1, SSM_LANES), full),
            pl.BlockSpec((2 * SSM_LANES, WIDTH), full),
            pl.BlockSpec((1, WIDTH), full),
            pl.BlockSpec((WIDTH, WIDTH), full),
            pl.BlockSpec((1, WIDTH), full),
        ],
        out_specs=pl.BlockSpec((rows, WIDTH), lambda c: (c, 0)),
        out_shape=jax.ShapeDtypeStruct((seq * bsz, WIDTH), BF16),
        scratch_shapes=[pltpu.VMEM((bsz, 2 * SSM_LANES), F32),
                        pltpu.VMEM((rows, 2 * SSM_LANES), F32)],
        compiler_params=pltpu.CompilerParams(
            dimension_semantics=("arbitrary",), vmem_limit_bytes=VMEM_LIMIT),
        name="s5_scan",
    )(du_rows, bbig, ar, ai, cbig, d_skip, w_glu, b_glu)


def _ssm_params(a_re, a_im, log_dt, b_re, b_im, c_re, c_im):
    dt = jnp.exp(log_dt)[:, None]
    mag = jnp.exp(dt * a_re)
    ab_re = mag * jnp.cos(dt * a_im)
    ab_im = mag * jnp.sin(dt * a_im)
    den = a_re * a_re + a_im * a_im
    nr, ni = ab_re - 1.0, ab_im
    coef_re = (nr * a_re + ni * a_im) / den
    coef_im = (ni * a_re - nr * a_im) / den
    bb_re = coef_re[..., None] * b_re - coef_im[..., None] * b_im
    bb_im = coef_re[..., None] * b_im + coef_im[..., None] * b_re
    eye = jnp.eye(SSM_GROUPS, dtype=F32)[:, None, :, None]

    def in_diag(m):
        return (eye * m.transpose(0, 2, 1)[:, :, None, :]).reshape(WIDTH, SSM_LANES)

    def out_diag(m):
        return (eye * m.transpose(0, 2, 1)[:, :, None, :]).reshape(SSM_LANES, WIDTH)

    bbig = jnp.concatenate([in_diag(bb_re), in_diag(bb_im)], axis=1).astype(BF16)
    cbig = jnp.concatenate([out_diag(c_re), -out_diag(c_im)], axis=0).astype(BF16)
    return bbig, ab_re.reshape(1, SSM_LANES), ab_im.reshape(1, SSM_LANES), cbig


def _merge_kernel(gate_ref, ab_ref, ac_ref, ax_ref, hc_ref, hx_ref, bu_ref, bv_ref,
                  yc_ref, yd_ref, x_ref, cw_ref, cb_ref, sgw_ref, sgb_ref, lng_ref, lnb_ref,
                  wbr_ref, wo_ref, o_ref, *, tm):
    s_idx = pl.program_id(1)

    z0 = ac_ref[...].astype(F32) * ax_ref[...].astype(F32)
    halo = hc_ref[...].astype(F32) * hx_ref[...].astype(F32)
    halo = jnp.where(s_idx == 0, jnp.zeros_like(halo), halo)
    hrows = halo.shape[0]
    h1 = halo[hrows - 1:hrows, :]
    h2 = halo[hrows - 2:hrows - 1, :]
    row = lax.broadcasted_iota(jnp.int32, z0.shape, 0)
    z1 = jnp.where(row == 0, h1, pltpu.roll(z0, 1, 0))
    z2 = jnp.where(row == 0, h2, jnp.where(row == 1, h1, pltpu.roll(z0, 2, 0)))
    conv = cw_ref[0:1, :] * z2 + cw_ref[1:2, :] * z1 + cw_ref[2:3, :] * z0 + cb_ref[...]
    y_a = ab_ref[...].astype(F32) * conv

    u = jax.nn.gelu(bu_ref[...].astype(F32))
    v = jax.nn.gelu(bv_ref[...].astype(F32))
    mu = jnp.mean(v, axis=-1, keepdims=True)
    vc = v - mu
    var = jnp.mean(vc * vc, axis=-1, keepdims=True)
    vn = (vc * lax.rsqrt(var + 1e-5) * lng_ref[...] + lnb_ref[...]).astype(BF16)
    tr = lax.broadcasted_iota(jnp.int32, (SG_CHUNK, SG_CHUNK), 0)
    tc = lax.broadcasted_iota(jnp.int32, (SG_CHUNK, SG_CHUNK), 1)
    gd = WIDTH // SG_GROUPS
    cols = []
    for g in range(SG_GROUPS):
        wg = jnp.where(tc <= tr, sgw_ref[g], 0.0).astype(BF16)
        bias = sgb_ref[:, g:g + 1]
        blocks = [jnp.dot(wg, vn[r * SG_CHUNK:(r + 1) * SG_CHUNK, g * gd:(g + 1) * gd],
                          preferred_element_type=F32) + bias
                  for r in range(tm // SG_CHUNK)]
        cols.append(jnp.concatenate(blocks, axis=0))
    y_b = u * jnp.concatenate(cols, axis=1)

    branches = (y_a.astype(BF16), y_b.astype(BF16), yc_ref[...], yd_ref[...])
    merged = None
    for n, y in enumerate(branches):
        gate = jax.nn.sigmoid(gate_ref[:, n * D_MODEL:(n + 1) * D_MODEL].astype(F32))
        term = gate * jnp.dot(y, wbr_ref[n], preferred_element_type=F32)
        merged = term if merged is None else merged + term
    o_ref[...] = x_ref[...] + jnp.dot(merged.astype(BF16), wo_ref[...],
                                      preferred_element_type=F32)


def _merge(proj3, y_c, y_d, x3, conv_w, conv_b, sg_w, sg_bt, ln_g, ln_b, w_br, w_o, tm):
    bsz, seq, _ = x3.shape
    hrows = 16
    hstep = tm // hrows

    def col(blk):
        return pl.BlockSpec((None, tm, WIDTH), lambda b, s: (b, s, blk))

    def halo(blk):
        return pl.BlockSpec((None, hrows, WIDTH),
                            lambda b, s: (b, jnp.maximum(s * hstep - 1, 0), blk))

    def full(shape):
        return pl.BlockSpec(shape, lambda b, s: (0,) * len(shape))

    return pl.pallas_call(
        functools.partial(_merge_kernel, tm=tm),
        grid=(bsz, seq // tm),
        in_specs=[
            pl.BlockSpec((None, tm, GATE_COLS), lambda b, s: (b, s, 0)),
            col(BLK_AB), col(BLK_AC), col(BLK_AX), halo(BLK_AC), halo(BLK_AX),
            col(BLK_BU), col(BLK_BV),
            pl.BlockSpec((None, tm, WIDTH), lambda b, s: (b, s, 0)),
            pl.BlockSpec((tm, WIDTH), lambda b, s: (s, b)),
            pl.BlockSpec((None, tm, D_MODEL), lambda b, s: (b, s, 0)),
            full((3, WIDTH)), full((1, WIDTH)),
            full((SG_GROUPS, SG_CHUNK, SG_CHUNK)), full((SG_CHUNK, SG_GROUPS)),
            full((1, WIDTH)), full((1, WIDTH)),
            full((N_BRANCH, WIDTH, D_MODEL)), full((D_MODEL, D_MODEL)),
        ],
        out_specs=pl.BlockSpec((None, tm, D_MODEL), lambda b, s: (b, s, 0)),
        out_shape=jax.ShapeDtypeStruct((bsz, seq, D_MODEL), F32),
        compiler_params=pltpu.CompilerParams(
            dimension_semantics=("parallel", "arbitrary"), vmem_limit_bytes=VMEM_LIMIT),
        name="merge",
    )(proj3, proj3, proj3, proj3, proj3, proj3, proj3, proj3, y_c, y_d, x3,
      conv_w, conv_b, sg_w, sg_bt, ln_g, ln_b, w_br, w_o)


def _ffn_kernel(x_ref, g_ref, wg_ref, wu_ref, wd_ref, gf_ref, o_ref, *, hchunk, final):
    x = x_ref[...]
    h = _rms(x, g_ref[...], 1e-6).astype(BF16)
    acc = x
    for c in range(FFN_HIDDEN // hchunk):
        cs = slice(c * hchunk, (c + 1) * hchunk)
        gate = jnp.dot(h, wg_ref[:, cs], preferred_element_type=F32)
        up = jnp.dot(h, wu_ref[:, cs], preferred_element_type=F32)
        act = (jax.nn.silu(gate) * up).astype(BF16)
        acc = acc + jnp.dot(act, wd_ref[cs, :], preferred_element_type=F32)
    if final:
        acc = _rms(acc, gf_ref[...], 1e-6)
    o_ref[...] = acc


def _ffn(x2d, g, w_gate, w_up, w_down, g_final, final, tm):
    n = x2d.shape[0]
    full = lambda i: (0, 0)
    return pl.pallas_call(
        functools.partial(_ffn_kernel, hchunk=FFN_HIDDEN // 2, final=final),
        grid=(n // tm,),
        in_specs=[
            pl.BlockSpec((tm, D_MODEL), lambda i: (i, 0)),
            pl.BlockSpec((1, D_MODEL), full),
            pl.BlockSpec((D_MODEL, FFN_HIDDEN), full),
            pl.BlockSpec((D_MODEL, FFN_HIDDEN), full),
            pl.BlockSpec((FFN_HIDDEN, D_MODEL), full),
            pl.BlockSpec((1, D_MODEL), full),
        ],
        out_specs=pl.BlockSpec((tm, D_MODEL), lambda i: (i, 0)),
        out_shape=jax.ShapeDtypeStruct((n, D_MODEL), F32),
        compiler_params=pltpu.CompilerParams(
            dimension_semantics=("parallel",), vmem_limit_bytes=VMEM_LIMIT),
        name="ffn",
    )(x2d, g, w_gate, w_up, w_down, g_final)


def _reorder_in_proj(w):
    return jnp.concatenate([w[:, -GATE_COLS:], w[:, :-GATE_COLS]], axis=1).astype(BF16)


def kernel(x, g_mix, w_in, conv_w, conv_b, sg_w, sg_b, sg_ln_g, sg_ln_b, lam_qk, subln_g,
           ssm_a_re, ssm_a_im, ssm_log_dt, ssm_b_re, ssm_b_im, ssm_c_re, ssm_c_im, ssm_d,
           w_glu, b_glu, w_br, w_o, g_ffn, w_ffn_gate, w_ffn_up, w_ffn_down, g_final):
    bsz, seq, _ = x.shape
    depth = w_in.shape[0]
    n = bsz * seq
    tm_proj = min(1024, seq)
    tm = min(512, seq)
    tq = min(256, seq)
    steps = min(64, seq)
    row2 = lambda a: a.reshape(1, -1)

    for l in range(depth):
        lam_init = 0.8 - 0.6 * math.exp(-0.3 * l)
        proj, du = _inproj(x.reshape(n, D_MODEL), row2(g_mix[l]), _reorder_in_proj(w_in[l]),
                           bsz, seq, tm_proj)
        proj3 = proj.reshape(bsz, seq, PROJ_COLS)

        y_c = _attention(proj3, lam_qk[l], row2(subln_g[l]), lam_init, tq)

        bbig, ar, ai, cbig = _ssm_params(ssm_a_re[l], ssm_a_im[l], ssm_log_dt[l],
                                         ssm_b_re[l], ssm_b_im[l], ssm_c_re[l], ssm_c_im[l])
        y_d = _ssm(du.reshape(seq * bsz, WIDTH), bbig, ar, ai, cbig, row2(ssm_d[l]),
                   w_glu[l].astype(BF16), row2(b_glu[l]), bsz, seq, steps)
        y_d = y_d.reshape(seq, bsz * WIDTH)

        x = _merge(proj3, y_c, y_d, x, conv_w[l], row2(conv_b[l]), sg_w[l], sg_b[l].T,
                   row2(sg_ln_g[l]), row2(sg_ln_b[l]), w_br[l].astype(BF16),
                   w_o[l].astype(BF16), tm)

        x = _ffn(x.reshape(n, D_MODEL), row2(g_ffn[l]), w_ffn_gate[l].astype(BF16),
                 w_ffn_up[l].astype(BF16), w_ffn_down[l].astype(BF16), row2(g_final),
                 l == depth - 1, tm).reshape(bsz, seq, D_MODEL)
    return x
```

```python
import functools
import math

import jax
import jax.numpy as jnp
from jax import lax
from jax.experimental import pallas as pl
from jax.experimental.pallas import tpu as pltpu

F32 = jnp.float32
BF16 = jnp.bfloat16

D_MODEL = 1024
WIDTH = 512
N_BRANCH = 4
CHUNK = 64
SG_CHUNK = 128
SG_GROUPS = 4
ATT_HEADS = 4
ATT_QK_DIM = 64
ATT_V_DIM = 128
SSM_GROUP = 16
SSM_GROUPS = 32
SSM_STATE = 64
SSM_LANES = SSM_GROUPS * SSM_STATE
FFN_HIDDEN = 2816

GATE_COLS = N_BRANCH * D_MODEL
BLK_AB, BLK_AC, BLK_AX, BLK_BU, BLK_BV, BLK_Q, BLK_K, BLK_V, BLK_DU = range(8, 17)
PROJ_BLOCKS = 16
PROJ_COLS = PROJ_BLOCKS * WIDTH

VMEM_LIMIT = 56 * 1024 * 1024


def _rms(x, g, eps):
    return x * lax.rsqrt(jnp.mean(x * x, axis=-1, keepdims=True) + eps) * g


def _inproj_kernel(x_ref, g_ref, w_ref, o_ref, du_ref):
    h = _rms(x_ref[...], g_ref[...], 1e-6).astype(BF16)
    for j in range(PROJ_BLOCKS + 1):
        cols = slice(j * WIDTH, (j + 1) * WIDTH)
        acc = jnp.dot(h, w_ref[:, cols], preferred_element_type=F32).astype(BF16)
        if j == BLK_DU:
            du_ref[...] = acc
        else:
            o_ref[:, cols] = acc


def _inproj(x2d, g, w, bsz, seq, tm):
    n = x2d.shape[0]
    spb = seq // tm
    return pl.pallas_call(
        _inproj_kernel,
        grid=(n // tm,),
        in_specs=[
            pl.BlockSpec((tm, D_MODEL), lambda i: (i, 0)),
            pl.BlockSpec((1, D_MODEL), lambda i: (0, 0)),
            pl.BlockSpec((D_MODEL, PROJ_COLS + WIDTH), lambda i: (0, 0),
                         pipeline_mode=pl.Buffered(1)),
        ],
        out_specs=[
            pl.BlockSpec((tm, PROJ_COLS), lambda i: (i, 0)),
            pl.BlockSpec((tm, WIDTH), lambda i: (i % spb, i // spb)),
        ],
        out_shape=[
            jax.ShapeDtypeStruct((n, PROJ_COLS), BF16),
            jax.ShapeDtypeStruct((seq, bsz * WIDTH), BF16),
        ],
        compiler_params=pltpu.CompilerParams(
            dimension_semantics=("parallel",), vmem_limit_bytes=VMEM_LIMIT),
        name="inproj",
    )(x2d, g, w)


ATT_ROWS = 16


def _attn_kernel(lam_ref, g_ref, q_ref, k_ref, v_ref, o_ref, vt_ref, qh_ref, s_ref, p_ref,
                 acc_ref, *, tq, lam_init):
    qi = pl.program_id(2)
    nchunk = tq // ATT_ROWS

    @pl.when(qi == 0)
    def _():
        for c in range(vt_ref.shape[0]):
            vt_ref[c] = v_ref[c * tq:(c + 1) * tq, :].T

    q = q_ref[...] * (ATT_QK_DIM ** -0.5)
    lane = lax.broadcasted_iota(jnp.int32, q.shape, 1)
    zero = jnp.zeros_like(q)
    qh_ref[0] = jnp.where(lane < ATT_QK_DIM, q, zero)
    qh_ref[1] = jnp.where(lane >= ATT_QK_DIM, q, zero)
    p_ref[1] = jnp.zeros(p_ref.shape[1:], BF16)
    acc_ref[...] = jnp.zeros_like(acc_ref)

    def scores(h, kb):
        k = k_ref[pl.ds(pl.multiple_of(kb * tq, tq), tq), :]
        s_ref[h] = lax.dot_general(k, qh_ref[h], (((1,), (1,)), ((), ())),
                                   preferred_element_type=F32)

    qlane = lax.broadcasted_iota(jnp.int32, (ATT_ROWS, tq), 1)

    def chunk(h, c, masked):
        x = s_ref[h, c * ATT_ROWS:(c + 1) * ATT_ROWS, :]
        if masked:
            x = jnp.where(qlane >= (c * ATT_ROWS) // CHUNK * CHUNK, x, -jnp.inf)
        return x

    def softmax(h, stats, masked):
        m, l = stats
        mx = chunk(h, 0, masked)
        for c in range(1, nchunk):
            mx = jnp.maximum(mx, chunk(h, c, masked))
        m_new = jnp.maximum(m, jnp.max(mx, axis=0, keepdims=True))
        alpha = jnp.exp(m - m_new)
        mb = jnp.broadcast_to(m_new, (ATT_ROWS, tq))
        tot = None
        for c in range(nchunk):
            p = jnp.exp(chunk(h, c, masked) - mb)
            tot = p if tot is None else tot + p
            p_ref[h, c * ATT_ROWS:(c + 1) * ATT_ROWS, :] = p.astype(BF16)
        l = alpha * l + jnp.sum(tot, axis=0, keepdims=True)
        return alpha, (m_new, l)

    def values(h, kb, alpha):
        acc_ref[h] = alpha * acc_ref[h] + jnp.dot(vt_ref[kb], p_ref[h],
                                                  preferred_element_type=F32)

    def body(kb, carry):
        st0, st1, alpha1 = carry
        values(1, jnp.maximum(kb - 1, 0), alpha1)
        scores(1, kb)
        alpha0, st0 = softmax(0, st0, False)
        values(0, kb, alpha0)
        scores(0, kb + 1)
        alpha1, st1 = softmax(1, st1, False)
        return st0, st1, alpha1

    scores(0, 0)
    stat0 = (jnp.full((1, tq), -jnp.inf, F32), jnp.zeros((1, tq), F32))
    st0, st1, alpha1 = lax.fori_loop(0, qi, body, (stat0, stat0, jnp.ones((1, tq), F32)))

    values(1, jnp.maximum(qi - 1, 0), alpha1)
    scores(1, qi)
    alpha0, (_, l0) = softmax(0, st0, True)
    values(0, qi, alpha0)
    alpha1, (_, l1) = softmax(1, st1, True)
    values(1, qi, alpha1)

    lf = lam_ref[...]
    lam = (jnp.exp(jnp.sum(lf[0:1] * lf[1:2], keepdims=True))
           - jnp.exp(jnp.sum(lf[2:3] * lf[3:4], keepdims=True)) + lam_init)
    o = acc_ref[0] / l0 - lam * (acc_ref[1] / l1)
    o = o * lax.rsqrt(jnp.mean(o * o, axis=0, keepdims=True) + 1e-5) * g_ref[...]
    o_ref[...] = (o * (1.0 - lam_init)).T.astype(o_ref.dtype)


def _attention(proj3, lam_qk, subln_g, lam_init, tq):
    bsz, seq, _ = proj3.shape
    lanes = ATT_V_DIM
    per = WIDTH // lanes
    return pl.pallas_call(
        functools.partial(_attn_kernel, tq=tq, lam_init=lam_init),
        grid=(bsz, ATT_HEADS, seq // tq),
        in_specs=[
            pl.BlockSpec((4, ATT_QK_DIM), lambda b, h, i: (0, 0)),
            pl.BlockSpec((ATT_V_DIM, 1), lambda b, h, i: (0, 0)),
            pl.BlockSpec((None, tq, lanes), lambda b, h, i: (b, i, BLK_Q * per + h)),
            pl.BlockSpec((None, seq, lanes), lambda b, h, i: (b, 0, BLK_K * per + h)),
            pl.BlockSpec((None, seq, lanes), lambda b, h, i: (b, 0, BLK_V * per + h)),
        ],
        out_specs=pl.BlockSpec((None, tq, lanes), lambda b, h, i: (b, i, h)),
        out_shape=jax.ShapeDtypeStruct((bsz, seq, WIDTH), BF16),
        scratch_shapes=[
            pltpu.VMEM((seq // tq, ATT_V_DIM, tq), BF16),
            pltpu.VMEM((2, tq, lanes), BF16),
            pltpu.VMEM((2, tq, tq), F32),
            pltpu.VMEM((2, tq, tq), BF16),
            pltpu.VMEM((2, ATT_V_DIM, tq), F32),
        ],
        compiler_params=pltpu.CompilerParams(
            dimension_semantics=("parallel", "parallel", "arbitrary"),
            vmem_limit_bytes=VMEM_LIMIT),
        name="diff_attention",
    )(lam_qk, subln_g, proj3, proj3, proj3)


SSM_BLOCKS = 4
SSM_BLOCK_IN = WIDTH // SSM_BLOCKS
SSM_SLAB = SSM_LANES // SSM_BLOCKS


def _ssm_kernel(u_ref, bblk_ref, ar_ref, ai_ref, cblk_ref, d_ref, wglu_ref, bglu_ref,
                o_ref, st_ref, bu_ref, *, bsz, steps):
    @pl.when(pl.program_id(0) == 0)
    def _():
        st_ref[...] = jnp.zeros_like(st_ref)

    slab = SSM_SLAB
    u = u_ref[...]
    for j in range(SSM_BLOCKS):
        bu_ref[:, 2 * slab * j:2 * slab * (j + 1)] = jnp.dot(
            u[:, SSM_BLOCK_IN * j:SSM_BLOCK_IN * (j + 1)], bblk_ref[j],
            preferred_element_type=F32)

    for j in range(SSM_BLOCKS):
        re = pl.ds(2 * slab * j, slab)
        im = pl.ds(2 * slab * j + slab, slab)
        ar = jnp.broadcast_to(ar_ref[:, slab * j:slab * (j + 1)], (bsz, slab))
        ai = jnp.broadcast_to(ai_ref[:, slab * j:slab * (j + 1)], (bsz, slab))

        def body(t, carry, re=re, im=im, ar=ar, ai=ai):
            xr, xi = carry
            rows = pl.ds(pl.multiple_of(t * bsz, bsz), bsz)
            nxr = ar * xr - ai * xi + bu_ref[rows, re]
            nxi = ar * xi + ai * xr + bu_ref[rows, im]
            bu_ref[rows, re] = nxr
            bu_ref[rows, im] = nxi
            return nxr, nxi

        xr, xi = lax.fori_loop(0, steps, body, (st_ref[:, re], st_ref[:, im]), unroll=4)
        st_ref[:, re] = xr
        st_ref[:, im] = xi

    y = jnp.concatenate(
        [jnp.dot(bu_ref[:, 2 * slab * j:2 * slab * (j + 1)].astype(BF16), cblk_ref[j],
                 preferred_element_type=F32) for j in range(SSM_BLOCKS)], axis=1)
    y = jax.nn.gelu(y + d_ref[...] * u.astype(F32))
    z = jnp.dot(y.astype(BF16), wglu_ref[...], preferred_element_type=F32) + bglu_ref[...]
    o_ref[...] = (y * jax.nn.sigmoid(z)).astype(o_ref.dtype)


def _ssm(du_rows, bblk, ar, ai, cblk, d_skip, w_glu, b_glu, bsz, seq, steps):
    rows = steps * bsz
    full = lambda c: (0, 0)
    return pl.pallas_call(
        functools.partial(_ssm_kernel, bsz=bsz, steps=steps),
        grid=(seq // steps,),
        in_specs=[
            pl.BlockSpec((rows, WIDTH), lambda c: (c, 0)),
            pl.BlockSpec((SSM_BLOCKS, SSM_BLOCK_IN, 2 * SSM_SLAB), lambda c: (0, 0, 0)),
            pl.BlockSpec((1, SSM_LANES), full),
            pl.BlockSpec((1, SSM_LANES), full),
            pl.BlockSpec((SSM_BLOCKS, 2 * SSM_SLAB, SSM_BLOCK_IN), lambda c: (0, 0, 0)),
            pl.BlockSpec((1, WIDTH), full),
            pl.BlockSpec((WIDTH, WIDTH), full),
            pl.BlockSpec((1, WIDTH), full),
        ],
        out_specs=pl.BlockSpec((rows, WIDTH), lambda c: (c, 0)),
        out_shape=jax.ShapeDtypeStruct((seq * bsz, WIDTH), BF16),
        scratch_shapes=[pltpu.VMEM((bsz, 2 * SSM_LANES), F32),
                        pltpu.VMEM((rows, 2 * SSM_LANES), F32)],
        compiler_params=pltpu.CompilerParams(
            dimension_semantics=("arbitrary",), vmem_limit_bytes=VMEM_LIMIT),
        name="s5_scan",
    )(du_rows, bblk, ar, ai, cblk, d_skip, w_glu, b_glu)


def _ssm_params(a_re, a_im, log_dt, b_re, b_im, c_re, c_im):
    dt = jnp.exp(log_dt)[:, None]
    mag = jnp.exp(dt * a_re)
    ab_re = mag * jnp.cos(dt * a_im)
    ab_im = mag * jnp.sin(dt * a_im)
    den = a_re * a_re + a_im * a_im
    nr, ni = ab_re - 1.0, ab_im
    coef_re = (nr * a_re + ni * a_im) / den
    coef_im = (ni * a_re - nr * a_im) / den
    bb_re = coef_re[..., None] * b_re - coef_im[..., None] * b_im
    bb_im = coef_re[..., None] * b_im + coef_im[..., None] * b_re
    per = SSM_GROUPS // SSM_BLOCKS
    eye = jnp.eye(per, dtype=F32)[None, :, None, :, None]

    def diag(m):
        a, b = m.shape[1:]
        mt = m.reshape(SSM_BLOCKS, per, a, b).transpose(0, 1, 3, 2)
        return (eye * mt[:, :, :, None, :]).reshape(SSM_BLOCKS, per * b, per * a)

    bblk = jnp.concatenate([diag(bb_re), diag(bb_im)], axis=2).astype(BF16)
    cblk = jnp.concatenate([diag(c_re), -diag(c_im)], axis=1).astype(BF16)
    return bblk, ab_re.reshape(1, SSM_LANES), ab_im.reshape(1, SSM_LANES), cblk


def _merge_kernel(gate_ref, ab_ref, ac_ref, ax_ref, hc_ref, hx_ref, bu_ref, bv_ref,
                  yc_ref, yd_ref, x_ref, cw_ref, cb_ref, sgw_ref, sgb_ref, lng_ref, lnb_ref,
                  wbr_ref, wo_ref, o_ref, *, tm):
    s_idx = pl.program_id(1)

    z0 = ac_ref[...].astype(F32) * ax_ref[...].astype(F32)
    halo = hc_ref[...].astype(F32) * hx_ref[...].astype(F32)
    halo = jnp.where(s_idx == 0, jnp.zeros_like(halo), halo)
    hrows = halo.shape[0]
    h1 = halo[hrows - 1:hrows, :]
    h2 = halo[hrows - 2:hrows - 1, :]
    row = lax.broadcasted_iota(jnp.int32, z0.shape, 0)
    z1 = jnp.where(row == 0, h1, pltpu.roll(z0, 1, 0))
    z2 = jnp.where(row == 0, h2, jnp.where(row == 1, h1, pltpu.roll(z0, 2, 0)))
    conv = cw_ref[0:1, :] * z2 + cw_ref[1:2, :] * z1 + cw_ref[2:3, :] * z0 + cb_ref[...]
    y_a = ab_ref[...].astype(F32) * conv

    u = jax.nn.gelu(bu_ref[...].astype(F32))
    v = jax.nn.gelu(bv_ref[...].astype(F32))
    mu = jnp.mean(v, axis=-1, keepdims=True)
    vc = v - mu
    var = jnp.mean(vc * vc, axis=-1, keepdims=True)
    vn = (vc * lax.rsqrt(var + 1e-5) * lng_ref[...] + lnb_ref[...]).astype(BF16)
    tr = lax.broadcasted_iota(jnp.int32, (SG_CHUNK, SG_CHUNK), 0)
    tc = lax.broadcasted_iota(jnp.int32, (SG_CHUNK, SG_CHUNK), 1)
    gd = WIDTH // SG_GROUPS
    cols = []
    for g in range(SG_GROUPS):
        wg = jnp.where(tc <= tr, sgw_ref[g], 0.0).astype(BF16)
        bias = sgb_ref[:, g:g + 1]
        blocks = [jnp.dot(wg, vn[r * SG_CHUNK:(r + 1) * SG_CHUNK, g * gd:(g + 1) * gd],
                          preferred_element_type=F32) + bias
                  for r in range(tm // SG_CHUNK)]
        cols.append(jnp.concatenate(blocks, axis=0))
    y_b = u * jnp.concatenate(cols, axis=1)

    branches = (y_a.astype(BF16), y_b.astype(BF16), yc_ref[...], yd_ref[...])
    merged = None
    for n, y in enumerate(branches):
        gate = jax.nn.sigmoid(gate_ref[:, n * D_MODEL:(n + 1) * D_MODEL].astype(F32))
        term = gate * jnp.dot(y, wbr_ref[n], preferred_element_type=F32)
        merged = term if merged is None else merged + term
    o_ref[...] = x_ref[...] + jnp.dot(merged.astype(BF16), wo_ref[...],
                                      preferred_element_type=F32)


def _merge(proj3, y_c, y_d, x3, conv_w, conv_b, sg_w, sg_bt, ln_g, ln_b, w_br, w_o, tm):
    bsz, seq, _ = x3.shape
    hrows = 16
    hstep = tm // hrows

    def col(blk):
        return pl.BlockSpec((None, tm, WIDTH), lambda b, s: (b, s, blk))

    def halo(blk):
        return pl.BlockSpec((None, hrows, WIDTH),
                            lambda b, s: (b, jnp.maximum(s * hstep - 1, 0), blk))

    def full(shape):
        return pl.BlockSpec(shape, lambda b, s: (0,) * len(shape))

    return pl.pallas_call(
        functools.partial(_merge_kernel, tm=tm),
        grid=(bsz, seq // tm),
        in_specs=[
            pl.BlockSpec((None, tm, GATE_COLS), lambda b, s: (b, s, 0)),
            col(BLK_AB), col(BLK_AC), col(BLK_AX), halo(BLK_AC), halo(BLK_AX),
            col(BLK_BU), col(BLK_BV),
            pl.BlockSpec((None, tm, WIDTH), lambda b, s: (b, s, 0)),
            pl.BlockSpec((tm, WIDTH), lambda b, s: (s, b)),
            pl.BlockSpec((None, tm, D_MODEL), lambda b, s: (b, s, 0)),
            full((3, WIDTH)), full((1, WIDTH)),
            full((SG_GROUPS, SG_CHUNK, SG_CHUNK)), full((SG_CHUNK, SG_GROUPS)),
            full((1, WIDTH)), full((1, WIDTH)),
            full((N_BRANCH, WIDTH, D_MODEL)), full((D_MODEL, D_MODEL)),
        ],
        out_specs=pl.BlockSpec((None, tm, D_MODEL), lambda b, s: (b, s, 0)),
        out_shape=jax.ShapeDtypeStruct((bsz, seq, D_MODEL), F32),
        compiler_params=pltpu.CompilerParams(
            dimension_semantics=("parallel", "arbitrary"), vmem_limit_bytes=VMEM_LIMIT),
        name="merge",
    )(proj3, proj3, proj3, proj3, proj3, proj3, proj3, proj3, y_c, y_d, x3,
      conv_w, conv_b, sg_w, sg_bt, ln_g, ln_b, w_br, w_o)


def _ffn_kernel(x_ref, g_ref, wg_ref, wu_ref, wd_ref, gf_ref, o_ref, *, hchunk, final):
    x = x_ref[...]
    h = _rms(x, g_ref[...], 1e-6).astype(BF16)
    acc = x
    for c in range(FFN_HIDDEN // hchunk):
        cs = slice(c * hchunk, (c + 1) * hchunk)
        gate = jnp.dot(h, wg_ref[:, cs], preferred_element_type=F32)
        up = jnp.dot(h, wu_ref[:, cs], preferred_element_type=F32)
        act = (jax.nn.silu(gate) * up).astype(BF16)
        acc = acc + jnp.dot(act, wd_ref[cs, :], preferred_element_type=F32)
    if final:
        acc = _rms(acc, gf_ref[...], 1e-6)
    o_ref[...] = acc


def _ffn(x2d, g, w_gate, w_up, w_down, g_final, final, tm):
    n = x2d.shape[0]
    full = lambda i: (0, 0)
    return pl.pallas_call(
        functools.partial(_ffn_kernel, hchunk=FFN_HIDDEN // 2, final=final),
        grid=(n // tm,),
        in_specs=[
            pl.BlockSpec((tm, D_MODEL), lambda i: (i, 0)),
            pl.BlockSpec((1, D_MODEL), full),
            pl.BlockSpec((D_MODEL, FFN_HIDDEN), full),
            pl.BlockSpec((D_MODEL, FFN_HIDDEN), full),
            pl.BlockSpec((FFN_HIDDEN, D_MODEL), full),
            pl.BlockSpec((1, D_MODEL), full),
        ],
        out_specs=pl.BlockSpec((tm, D_MODEL), lambda i: (i, 0)),
        out_shape=jax.ShapeDtypeStruct((n, D_MODEL), F32),
        compiler_params=pltpu.CompilerParams(
            dimension_semantics=("parallel",), vmem_limit_bytes=VMEM_LIMIT),
        name="ffn",
    )(x2d, g, w_gate, w_up, w_down, g_final)


def _reorder_in_proj(w):
    return jnp.concatenate([w[:, -GATE_COLS:], w[:, :-GATE_COLS]], axis=1).astype(BF16)


def kernel(x, g_mix, w_in, conv_w, conv_b, sg_w, sg_b, sg_ln_g, sg_ln_b, lam_qk, subln_g,
           ssm_a_re, ssm_a_im, ssm_log_dt, ssm_b_re, ssm_b_im, ssm_c_re, ssm_c_im, ssm_d,
           w_glu, b_glu, w_br, w_o, g_ffn, w_ffn_gate, w_ffn_up, w_ffn_down, g_final):
    bsz, seq, _ = x.shape
    depth = w_in.shape[0]
    n = bsz * seq
    tm = min(512, seq)
    tq = min(512, seq)
    steps = min(64, seq)
    row2 = lambda a: a.reshape(1, -1)

    for l in range(depth):
        lam_init = 0.8 - 0.6 * math.exp(-0.3 * l)
        proj, du = _inproj(x.reshape(n, D_MODEL), row2(g_mix[l]), _reorder_in_proj(w_in[l]),
                           bsz, seq, tm)
        proj3 = proj.reshape(bsz, seq, PROJ_COLS)

        y_c = _attention(proj3, lam_qk[l], subln_g[l].reshape(-1, 1), lam_init, tq)

        bblk, ar, ai, cblk = _ssm_params(ssm_a_re[l], ssm_a_im[l], ssm_log_dt[l],
                                         ssm_b_re[l], ssm_b_im[l], ssm_c_re[l], ssm_c_im[l])
        y_d = _ssm(du.reshape(seq * bsz, WIDTH), bblk, ar, ai, cblk, row2(ssm_d[l]),
                   w_glu[l].astype(BF16), row2(b_glu[l]), bsz, seq, steps)
        y_d = y_d.reshape(seq, bsz * WIDTH)

        x = _merge(proj3, y_c, y_d, x, conv_w[l], row2(conv_b[l]), sg_w[l], sg_b[l].T,
                   row2(sg_ln_g[l]), row2(sg_ln_b[l]), w_br[l].astype(BF16),
                   w_o[l].astype(BF16), tm)

        x = _ffn(x.reshape(n, D_MODEL), row2(g_ffn[l]), w_ffn_gate[l].astype(BF16),
                 w_ffn_up[l].astype(BF16), w_ffn_down[l].astype(BF16), row2(g_final),
                 l == depth - 1, tm).reshape(bsz, seq, D_MODEL)
    return x
```

```python
import functools
import math

import jax
import jax.numpy as jnp
from jax import lax
from jax.experimental import pallas as pl
from jax.experimental.pallas import tpu as pltpu

F32 = jnp.float32
BF16 = jnp.bfloat16

D_MODEL = 1024
WIDTH = 512
N_BRANCH = 4
CHUNK = 64
SG_CHUNK = 128
SG_GROUPS = 4
ATT_HEADS = 4
ATT_QK_DIM = 64
ATT_V_DIM = 128
SSM_GROUP = 16
SSM_GROUPS = 32
SSM_STATE = 64
SSM_LANES = SSM_GROUPS * SSM_STATE
FFN_HIDDEN = 2816

GATE_COLS = N_BRANCH * D_MODEL
BLK_AB, BLK_AC, BLK_AX, BLK_BU, BLK_BV, BLK_Q, BLK_K, BLK_V, BLK_DU = range(8, 17)
PROJ_BLOCKS = 13
PROJ_COLS = PROJ_BLOCKS * WIDTH
IN_BLOCKS = 17
SSM_STEPS = 64

VMEM_LIMIT = 56 * 1024 * 1024


def _rms(x, g, eps):
    return x * lax.rsqrt(jnp.mean(x * x, axis=-1, keepdims=True) + eps) * g


def _inproj_kernel(x_ref, g_ref, w_ref, o_ref, qkv_ref, du_ref):
    h = _rms(x_ref[...], g_ref[...], 1e-6).astype(BF16)
    for j in range(IN_BLOCKS):
        cols = slice(j * WIDTH, (j + 1) * WIDTH)
        acc = jnp.dot(h, w_ref[:, cols], preferred_element_type=F32).astype(BF16)
        if j == BLK_DU:
            du_ref[...] = acc.reshape(du_ref.shape)
        elif j in (BLK_Q, BLK_K, BLK_V):
            for hd in range(ATT_HEADS):
                qkv_ref[j - BLK_Q, hd] = acc[:, hd * ATT_V_DIM:(hd + 1) * ATT_V_DIM]
        else:
            o_ref[:, cols] = acc


def _inproj(x2d, g, w, bsz, seq, tm):
    n = x2d.shape[0]
    spb = seq // tm
    return pl.pallas_call(
        _inproj_kernel,
        grid=(n // tm,),
        in_specs=[
            pl.BlockSpec((tm, D_MODEL), lambda i: (i, 0)),
            pl.BlockSpec((1, D_MODEL), lambda i: (0, 0)),
            pl.BlockSpec((D_MODEL, IN_BLOCKS * WIDTH), lambda i: (0, 0),
                         pipeline_mode=pl.Buffered(1)),
        ],
        out_specs=[
            pl.BlockSpec((tm, PROJ_COLS), lambda i: (i, 0)),
            pl.BlockSpec((3, None, ATT_HEADS, tm, ATT_V_DIM),
                         lambda i: (0, i // spb, 0, i % spb, 0)),
            pl.BlockSpec((tm // SSM_STEPS, SSM_STEPS, WIDTH), lambda i: (i % spb, i // spb, 0)),
        ],
        out_shape=[
            jax.ShapeDtypeStruct((n, PROJ_COLS), BF16),
            jax.ShapeDtypeStruct((3, bsz, ATT_HEADS, seq, ATT_V_DIM), BF16),
            jax.ShapeDtypeStruct((seq // SSM_STEPS, bsz * SSM_STEPS, WIDTH), BF16),
        ],
        compiler_params=pltpu.CompilerParams(
            dimension_semantics=("parallel",), vmem_limit_bytes=VMEM_LIMIT),
        name="inproj",
    )(x2d, g, w)


ATT_ROWS = 16


def _attn_kernel(lam_ref, g_ref, q_ref, k_ref, v_ref, o_ref, vt_ref, qh_ref, s_ref, p_ref,
                 acc_ref, *, tq, lam_init):
    qi = pl.program_id(2)
    nchunk = tq // ATT_ROWS

    @pl.when(qi == 0)
    def _():
        for c in range(vt_ref.shape[0]):
            vt_ref[c] = v_ref[c * tq:(c + 1) * tq, :].T

    q = q_ref[...] * (ATT_QK_DIM ** -0.5)
    lane = lax.broadcasted_iota(jnp.int32, q.shape, 1)
    zero = jnp.zeros_like(q)
    qh_ref[0] = jnp.where(lane < ATT_QK_DIM, q, zero)
    qh_ref[1] = jnp.where(lane >= ATT_QK_DIM, q, zero)
    p_ref[1] = jnp.zeros(p_ref.shape[1:], BF16)
    acc_ref[...] = jnp.zeros_like(acc_ref)

    def scores(h, kb):
        k = k_ref[pl.ds(pl.multiple_of(kb * tq, tq), tq), :]
        s_ref[h] = lax.dot_general(k, qh_ref[h], (((1,), (1,)), ((), ())),
                                   preferred_element_type=F32)

    qlane = lax.broadcasted_iota(jnp.int32, (ATT_ROWS, tq), 1)

    def chunk(h, c, masked):
        x = s_ref[h, c * ATT_ROWS:(c + 1) * ATT_ROWS, :]
        if masked:
            x = jnp.where(qlane >= (c * ATT_ROWS) // CHUNK * CHUNK, x, -jnp.inf)
        return x

    def softmax(h, stats, masked):
        m, l = stats
        mx = chunk(h, 0, masked)
        for c in range(1, nchunk):
            mx = jnp.maximum(mx, chunk(h, c, masked))
        m_new = jnp.maximum(m, jnp.max(mx, axis=0, keepdims=True))
        alpha = jnp.exp(m - m_new)
        mb = jnp.broadcast_to(m_new, (ATT_ROWS, tq))
        tot = None
        for c in range(nchunk):
            p = jnp.exp(chunk(h, c, masked) - mb)
            tot = p if tot is None else tot + p
            p_ref[h, c * ATT_ROWS:(c + 1) * ATT_ROWS, :] = p.astype(BF16)
        l = alpha * l + jnp.sum(tot, axis=0, keepdims=True)
        return alpha, (m_new, l)

    def values(h, kb, alpha):
        acc_ref[h] = alpha * acc_ref[h] + jnp.dot(vt_ref[kb], p_ref[h],
                                                  preferred_element_type=F32)

    def body(kb, carry):
        st0, st1, alpha1 = carry
        values(1, jnp.maximum(kb - 1, 0), alpha1)
        scores(1, kb)
        alpha0, st0 = softmax(0, st0, False)
        values(0, kb, alpha0)
        scores(0, kb + 1)
        alpha1, st1 = softmax(1, st1, False)
        return st0, st1, alpha1

    scores(0, 0)
    stat0 = (jnp.full((1, tq), -jnp.inf, F32), jnp.zeros((1, tq), F32))
    st0, st1, alpha1 = lax.fori_loop(0, qi, body, (stat0, stat0, jnp.ones((1, tq), F32)))

    values(1, jnp.maximum(qi - 1, 0), alpha1)
    scores(1, qi)
    alpha0, (_, l0) = softmax(0, st0, True)
    values(0, qi, alpha0)
    alpha1, (_, l1) = softmax(1, st1, True)
    values(1, qi, alpha1)

    lf = lam_ref[...]
    lam = (jnp.exp(jnp.sum(lf[0:1] * lf[1:2], keepdims=True))
           - jnp.exp(jnp.sum(lf[2:3] * lf[3:4], keepdims=True)) + lam_init)
    o = acc_ref[0] / l0 - lam * (acc_ref[1] / l1)
    o = o * lax.rsqrt(jnp.mean(o * o, axis=0, keepdims=True) + 1e-5) * g_ref[...]
    o_ref[...] = (o * (1.0 - lam_init)).T.astype(o_ref.dtype)


def _attention(qkv, lam_qk, subln_g, lam_init, tq):
    _, bsz, _, seq, lanes = qkv.shape
    return pl.pallas_call(
        functools.partial(_attn_kernel, tq=tq, lam_init=lam_init),
        grid=(bsz, ATT_HEADS, seq // tq),
        in_specs=[
            pl.BlockSpec((4, ATT_QK_DIM), lambda b, h, i: (0, 0)),
            pl.BlockSpec((ATT_V_DIM, 1), lambda b, h, i: (0, 0)),
            pl.BlockSpec((None, None, None, tq, lanes), lambda b, h, i: (0, b, h, i, 0)),
            pl.BlockSpec((None, None, None, seq, lanes), lambda b, h, i: (1, b, h, 0, 0)),
            pl.BlockSpec((None, None, None, seq, lanes), lambda b, h, i: (2, b, h, 0, 0)),
        ],
        out_specs=pl.BlockSpec((None, None, tq, lanes), lambda b, h, i: (b, h, i, 0)),
        out_shape=jax.ShapeDtypeStruct((bsz, ATT_HEADS, seq, lanes), BF16),
        scratch_shapes=[
            pltpu.VMEM((seq // tq, ATT_V_DIM, tq), BF16),
            pltpu.VMEM((2, tq, lanes), BF16),
            pltpu.VMEM((2, tq, tq), F32),
            pltpu.VMEM((2, tq, tq), BF16),
            pltpu.VMEM((2, ATT_V_DIM, tq), F32),
        ],
        compiler_params=pltpu.CompilerParams(
            dimension_semantics=("parallel", "parallel", "arbitrary"),
            vmem_limit_bytes=VMEM_LIMIT),
        name="diff_attention",
    )(lam_qk, subln_g, qkv, qkv, qkv)


SSM_BLOCKS = 4
SSM_BLOCK_IN = WIDTH // SSM_BLOCKS
SSM_SLAB = SSM_LANES // SSM_BLOCKS


def _ssm_kernel(u_ref, perm_ref, bblk_ref, ar_ref, ai_ref, cblk_ref, d_ref, wglu_ref, bglu_ref,
                o_ref, st_ref, bu_ref, *, bsz, steps):
    @pl.when(pl.program_id(0) == 0)
    def _():
        st_ref[...] = jnp.zeros_like(st_ref)

    slab = SSM_SLAB
    perm = perm_ref[...]
    u = jnp.dot(perm, u_ref[...], preferred_element_type=F32).astype(BF16)
    for j in range(SSM_BLOCKS):
        bu_ref[:, 2 * slab * j:2 * slab * (j + 1)] = jnp.dot(
            u[:, SSM_BLOCK_IN * j:SSM_BLOCK_IN * (j + 1)], bblk_ref[j],
            preferred_element_type=F32)

    for j in range(SSM_BLOCKS):
        re = pl.ds(2 * slab * j, slab)
        im = pl.ds(2 * slab * j + slab, slab)
        ar = jnp.broadcast_to(ar_ref[:, slab * j:slab * (j + 1)], (bsz, slab))
        ai = jnp.broadcast_to(ai_ref[:, slab * j:slab * (j + 1)], (bsz, slab))

        def body(t, carry, re=re, im=im, ar=ar, ai=ai):
            xr, xi = carry
            rows = pl.ds(pl.multiple_of(t * bsz, bsz), bsz)
            nxr = ar * xr - ai * xi + bu_ref[rows, re]
            nxi = ar * xi + ai * xr + bu_ref[rows, im]
            bu_ref[rows, re] = nxr
            bu_ref[rows, im] = nxi
            return nxr, nxi

        xr, xi = lax.fori_loop(0, steps, body, (st_ref[:, re], st_ref[:, im]), unroll=4)
        st_ref[:, re] = xr
        st_ref[:, im] = xi

    y = jnp.concatenate(
        [jnp.dot(bu_ref[:, 2 * slab * j:2 * slab * (j + 1)].astype(BF16), cblk_ref[j],
                 preferred_element_type=F32) for j in range(SSM_BLOCKS)], axis=1)
    y = jax.nn.gelu(y + d_ref[...] * u.astype(F32))
    z = jnp.dot(y.astype(BF16), wglu_ref[...], preferred_element_type=F32) + bglu_ref[...]
    out = (y * jax.nn.sigmoid(z)).astype(BF16)
    o_ref[...] = lax.dot_general(perm, out, (((0,), (0,)), ((), ())),
                                 preferred_element_type=F32).astype(o_ref.dtype)


def _ssm(du_rows, bblk, ar, ai, cblk, d_skip, w_glu, b_glu, bsz, seq, steps):
    rows = steps * bsz
    full = lambda c: (0, 0)
    r_out = jnp.arange(rows)[:, None]
    r_in = jnp.arange(rows)[None, :]
    perm = (r_in == (r_out % bsz) * steps + r_out // bsz).astype(BF16)
    return pl.pallas_call(
        functools.partial(_ssm_kernel, bsz=bsz, steps=steps),
        grid=(seq // steps,),
        in_specs=[
            pl.BlockSpec((None, rows, WIDTH), lambda c: (c, 0, 0)),
            pl.BlockSpec((rows, rows), full),
            pl.BlockSpec((SSM_BLOCKS, SSM_BLOCK_IN, 2 * SSM_SLAB), lambda c: (0, 0, 0)),
            pl.BlockSpec((1, SSM_LANES), full),
            pl.BlockSpec((1, SSM_LANES), full),
            pl.BlockSpec((SSM_BLOCKS, 2 * SSM_SLAB, SSM_BLOCK_IN), lambda c: (0, 0, 0)),
            pl.BlockSpec((1, WIDTH), full),
            pl.BlockSpec((WIDTH, WIDTH), full),
            pl.BlockSpec((1, WIDTH), full),
        ],
        out_specs=pl.BlockSpec((None, rows, WIDTH), lambda c: (c, 0, 0)),
        out_shape=jax.ShapeDtypeStruct((seq // steps, rows, WIDTH), BF16),
        scratch_shapes=[pltpu.VMEM((bsz, 2 * SSM_LANES), F32),
                        pltpu.VMEM((rows, 2 * SSM_LANES), F32)],
        compiler_params=pltpu.CompilerParams(
            dimension_semantics=("arbitrary",), vmem_limit_bytes=VMEM_LIMIT),
        name="s5_scan",
    )(du_rows, perm, bblk, ar, ai, cblk, d_skip, w_glu, b_glu)


def _ssm_params(a_re, a_im, log_dt, b_re, b_im, c_re, c_im):
    dt = jnp.exp(log_dt)[:, None]
    mag = jnp.exp(dt * a_re)
    ab_re = mag * jnp.cos(dt * a_im)
    ab_im = mag * jnp.sin(dt * a_im)
    den = a_re * a_re + a_im * a_im
    nr, ni = ab_re - 1.0, ab_im
    coef_re = (nr * a_re + ni * a_im) / den
    coef_im = (ni * a_re - nr * a_im) / den
    bb_re = coef_re[..., None] * b_re - coef_im[..., None] * b_im
    bb_im = coef_re[..., None] * b_im + coef_im[..., None] * b_re
    per = SSM_GROUPS // SSM_BLOCKS
    eye = jnp.eye(per, dtype=F32)[None, :, None, :, None]

    def diag(m):
        a, b = m.shape[1:]
        mt = m.reshape(SSM_BLOCKS, per, a, b).transpose(0, 1, 3, 2)
        return (eye * mt[:, :, :, None, :]).reshape(SSM_BLOCKS, per * b, per * a)

    bblk = jnp.concatenate([diag(bb_re), diag(bb_im)], axis=2).astype(BF16)
    cblk = jnp.concatenate([diag(c_re), -diag(c_im)], axis=1).astype(BF16)
    return bblk, ab_re.reshape(1, SSM_LANES), ab_im.reshape(1, SSM_LANES), cblk


def _merge_kernel(gate_ref, ab_ref, ac_ref, ax_ref, hc_ref, hx_ref, bu_ref, bv_ref,
                  yc_ref, yd_ref, x_ref, cw_ref, cb_ref, sgw_ref, sgb_ref, lng_ref, lnb_ref,
                  wbr_ref, wo_ref, o_ref, *, tm):
    s_idx = pl.program_id(1)

    z0 = ac_ref[...].astype(F32) * ax_ref[...].astype(F32)
    halo = hc_ref[...].astype(F32) * hx_ref[...].astype(F32)
    halo = jnp.where(s_idx == 0, jnp.zeros_like(halo), halo)
    hrows = halo.shape[0]
    h1 = halo[hrows - 1:hrows, :]
    h2 = halo[hrows - 2:hrows - 1, :]
    row = lax.broadcasted_iota(jnp.int32, z0.shape, 0)
    z1 = jnp.where(row == 0, h1, pltpu.roll(z0, 1, 0))
    z2 = jnp.where(row == 0, h2, jnp.where(row == 1, h1, pltpu.roll(z0, 2, 0)))
    conv = cw_ref[0:1, :] * z2 + cw_ref[1:2, :] * z1 + cw_ref[2:3, :] * z0 + cb_ref[...]
    y_a = ab_ref[...].astype(F32) * conv

    u = jax.nn.gelu(bu_ref[...].astype(F32))
    v = jax.nn.gelu(bv_ref[...].astype(F32))
    mu = jnp.mean(v, axis=-1, keepdims=True)
    vc = v - mu
    var = jnp.mean(vc * vc, axis=-1, keepdims=True)
    vn = (vc * lax.rsqrt(var + 1e-5) * lng_ref[...] + lnb_ref[...]).astype(BF16)
    tr = lax.broadcasted_iota(jnp.int32, (SG_CHUNK, SG_CHUNK), 0)
    tc = lax.broadcasted_iota(jnp.int32, (SG_CHUNK, SG_CHUNK), 1)
    gd = WIDTH // SG_GROUPS
    cols = []
    for g in range(SG_GROUPS):
        wg = jnp.where(tc <= tr, sgw_ref[g], 0.0).astype(BF16)
        bias = sgb_ref[:, g:g + 1]
        blocks = [jnp.dot(wg, vn[r * SG_CHUNK:(r + 1) * SG_CHUNK, g * gd:(g + 1) * gd],
                          preferred_element_type=F32) + bias
                  for r in range(tm // SG_CHUNK)]
        cols.append(jnp.concatenate(blocks, axis=0))
    y_b = u * jnp.concatenate(cols, axis=1)

    y_c = jnp.concatenate([yc_ref[hd] for hd in range(ATT_HEADS)], axis=1)
    y_d = yd_ref[...].reshape(tm, WIDTH)
    branches = (y_a.astype(BF16), y_b.astype(BF16), y_c, y_d)
    merged = None
    for n, y in enumerate(branches):
        gate = jax.nn.sigmoid(gate_ref[:, n * D_MODEL:(n + 1) * D_MODEL].astype(F32))
        term = gate * jnp.dot(y, wbr_ref[n], preferred_element_type=F32)
        merged = term if merged is None else merged + term
    o_ref[...] = x_ref[...] + jnp.dot(merged.astype(BF16), wo_ref[...],
                                      preferred_element_type=F32)


def _merge(proj3, y_c, y_d, x3, conv_w, conv_b, sg_w, sg_bt, ln_g, ln_b, w_br, w_o, tm):
    bsz, seq, _ = x3.shape
    hrows = 16
    hstep = tm // hrows

    def col(blk):
        return pl.BlockSpec((None, tm, WIDTH), lambda b, s: (b, s, blk))

    def halo(blk):
        return pl.BlockSpec((None, hrows, WIDTH),
                            lambda b, s: (b, jnp.maximum(s * hstep - 1, 0), blk))

    def full(shape):
        return pl.BlockSpec(shape, lambda b, s: (0,) * len(shape))

    return pl.pallas_call(
        functools.partial(_merge_kernel, tm=tm),
        grid=(bsz, seq // tm),
        in_specs=[
            pl.BlockSpec((None, tm, GATE_COLS), lambda b, s: (b, s, 0)),
            col(BLK_AB), col(BLK_AC), col(BLK_AX), halo(BLK_AC), halo(BLK_AX),
            col(BLK_BU), col(BLK_BV),
            pl.BlockSpec((None, ATT_HEADS, tm, ATT_V_DIM), lambda b, s: (b, 0, s, 0)),
            pl.BlockSpec((tm // SSM_STEPS, SSM_STEPS, WIDTH), lambda b, s: (s, b, 0)),
            pl.BlockSpec((None, tm, D_MODEL), lambda b, s: (b, s, 0)),
            full((3, WIDTH)), full((1, WIDTH)),
            full((SG_GROUPS, SG_CHUNK, SG_CHUNK)), full((SG_CHUNK, SG_GROUPS)),
            full((1, WIDTH)), full((1, WIDTH)),
            full((N_BRANCH, WIDTH, D_MODEL)), full((D_MODEL, D_MODEL)),
        ],
        out_specs=pl.BlockSpec((None, tm, D_MODEL), lambda b, s: (b, s, 0)),
        out_shape=jax.ShapeDtypeStruct((bsz, seq, D_MODEL), F32),
        compiler_params=pltpu.CompilerParams(
            dimension_semantics=("parallel", "arbitrary"), vmem_limit_bytes=VMEM_LIMIT),
        name="merge",
    )(proj3, proj3, proj3, proj3, proj3, proj3, proj3, proj3, y_c, y_d, x3,
      conv_w, conv_b, sg_w, sg_bt, ln_g, ln_b, w_br, w_o)


def _ffn_kernel(x_ref, g_ref, wg_ref, wu_ref, wd_ref, gf_ref, o_ref, *, hchunk, final):
    x = x_ref[...]
    h = _rms(x, g_ref[...], 1e-6).astype(BF16)
    acc = x
    for c in range(FFN_HIDDEN // hchunk):
        cs = slice(c * hchunk, (c + 1) * hchunk)
        gate = jnp.dot(h, wg_ref[:, cs], preferred_element_type=F32)
        up = jnp.dot(h, wu_ref[:, cs], preferred_element_type=F32)
        act = (jax.nn.silu(gate) * up).astype(BF16)
        acc = acc + jnp.dot(act, wd_ref[cs, :], preferred_element_type=F32)
    if final:
        acc = _rms(acc, gf_ref[...], 1e-6)
    o_ref[...] = acc


def _ffn(x2d, g, w_gate, w_up, w_down, g_final, final, tm):
    n = x2d.shape[0]
    full = lambda i: (0, 0)
    return pl.pallas_call(
        functools.partial(_ffn_kernel, hchunk=FFN_HIDDEN // 2, final=final),
        grid=(n // tm,),
        in_specs=[
            pl.BlockSpec((tm, D_MODEL), lambda i: (i, 0)),
            pl.BlockSpec((1, D_MODEL), full),
            pl.BlockSpec((D_MODEL, FFN_HIDDEN), full),
            pl.BlockSpec((D_MODEL, FFN_HIDDEN), full),
            pl.BlockSpec((FFN_HIDDEN, D_MODEL), full),
            pl.BlockSpec((1, D_MODEL), full),
        ],
        out_specs=pl.BlockSpec((tm, D_MODEL), lambda i: (i, 0)),
        out_shape=jax.ShapeDtypeStruct((n, D_MODEL), F32),
        compiler_params=pltpu.CompilerParams(
            dimension_semantics=("parallel",), vmem_limit_bytes=VMEM_LIMIT),
        name="ffn",
    )(x2d, g, w_gate, w_up, w_down, g_final)


def _reorder_in_proj(w):
    return jnp.concatenate([w[:, -GATE_COLS:], w[:, :-GATE_COLS]], axis=1).astype(BF16)


def kernel(x, g_mix, w_in, conv_w, conv_b, sg_w, sg_b, sg_ln_g, sg_ln_b, lam_qk, subln_g,
           ssm_a_re, ssm_a_im, ssm_log_dt, ssm_b_re, ssm_b_im, ssm_c_re, ssm_c_im, ssm_d,
           w_glu, b_glu, w_br, w_o, g_ffn, w_ffn_gate, w_ffn_up, w_ffn_down, g_final):
    bsz, seq, _ = x.shape
    depth = w_in.shape[0]
    n = bsz * seq
    tm = min(512, seq)
    tq = min(512, seq)
    steps = SSM_STEPS
    row2 = lambda a: a.reshape(1, -1)

    for l in range(depth):
        lam_init = 0.8 - 0.6 * math.exp(-0.3 * l)
        proj, qkv, du = _inproj(x.reshape(n, D_MODEL), row2(g_mix[l]), _reorder_in_proj(w_in[l]),
                                bsz, seq, tm)
        proj3 = proj.reshape(bsz, seq, PROJ_COLS)

        y_c = _attention(qkv, lam_qk[l], subln_g[l].reshape(-1, 1), lam_init, tq)

        bblk, ar, ai, cblk = _ssm_params(ssm_a_re[l], ssm_a_im[l], ssm_log_dt[l],
                                         ssm_b_re[l], ssm_b_im[l], ssm_c_re[l], ssm_c_im[l])
        y_d = _ssm(du, bblk, ar, ai, cblk, row2(ssm_d[l]),
                   w_glu[l].astype(BF16), row2(b_glu[l]), bsz, seq, steps)

        x = _merge(proj3, y_c, y_d, x, conv_w[l], row2(conv_b[l]), sg_w[l], sg_b[l].T,
                   row2(sg_ln_g[l]), row2(sg_ln_b[l]), w_br[l].astype(BF16),
                   w_o[l].astype(BF16), tm)

        x = _ffn(x.reshape(n, D_MODEL), row2(g_ffn[l]), w_ffn_gate[l].astype(BF16),
                 w_ffn_up[l].astype(BF16), w_ffn_down[l].astype(BF16), row2(g_final),
                 l == depth - 1, tm).reshape(bsz, seq, D_MODEL)
    return x
```

```python
import functools
import math

import jax
import jax.numpy as jnp
from jax import lax
from jax.experimental import pallas as pl
from jax.experimental.pallas import tpu as pltpu

F32 = jnp.float32
BF16 = jnp.bfloat16

D_MODEL = 1024
WIDTH = 512
N_BRANCH = 4
CHUNK = 64
SG_CHUNK = 128
SG_GROUPS = 4
ATT_HEADS = 4
ATT_QK_DIM = 64
ATT_V_DIM = 128
SSM_GROUP = 16
SSM_GROUPS = 32
SSM_STATE = 64
SSM_LANES = SSM_GROUPS * SSM_STATE
FFN_HIDDEN = 2816

GATE_COLS = N_BRANCH * D_MODEL
BLK_AB, BLK_AC, BLK_AX, BLK_BU, BLK_BV, BLK_Q, BLK_K, BLK_V, BLK_DU = range(8, 17)
PROJ_BLOCKS = 13
PROJ_COLS = PROJ_BLOCKS * WIDTH
IN_BLOCKS = 17
SSM_STEPS = 64

VMEM_LIMIT = 56 * 1024 * 1024


def _rms(x, g, eps):
    return x * lax.rsqrt(jnp.mean(x * x, axis=-1, keepdims=True) + eps) * g


def _inproj_kernel(x_ref, g_ref, w_ref, o_ref, qkv_ref, du_ref):
    h = _rms(x_ref[...], g_ref[...], 1e-6).astype(BF16)
    for j in range(IN_BLOCKS):
        cols = slice(j * WIDTH, (j + 1) * WIDTH)
        acc = jnp.dot(h, w_ref[:, cols], preferred_element_type=F32).astype(BF16)
        if j == BLK_DU:
            du_ref[...] = acc.reshape(du_ref.shape)
        elif j in (BLK_Q, BLK_K, BLK_V):
            for hd in range(ATT_HEADS):
                qkv_ref[j - BLK_Q, hd] = acc[:, hd * ATT_V_DIM:(hd + 1) * ATT_V_DIM]
        else:
            o_ref[:, cols] = acc


def _inproj(x2d, g, w, layer, bsz, seq, tm):
    n = x2d.shape[0]
    spb = seq // tm
    return pl.pallas_call(
        _inproj_kernel,
        grid=(n // tm,),
        in_specs=[
            pl.BlockSpec((tm, D_MODEL), lambda i: (i, 0)),
            pl.BlockSpec((1, D_MODEL), lambda i: (0, 0)),
            pl.BlockSpec((None, D_MODEL, IN_BLOCKS * WIDTH), lambda i: (layer, 0, 0),
                         pipeline_mode=pl.Buffered(1)),
        ],
        out_specs=[
            pl.BlockSpec((tm, PROJ_COLS), lambda i: (i, 0)),
            pl.BlockSpec((3, None, ATT_HEADS, tm, ATT_V_DIM),
                         lambda i: (0, i // spb, 0, i % spb, 0)),
            pl.BlockSpec((tm // SSM_STEPS, SSM_STEPS, WIDTH), lambda i: (i % spb, i // spb, 0)),
        ],
        out_shape=[
            jax.ShapeDtypeStruct((n, PROJ_COLS), BF16),
            jax.ShapeDtypeStruct((3, bsz, ATT_HEADS, seq, ATT_V_DIM), BF16),
            jax.ShapeDtypeStruct((seq // SSM_STEPS, bsz * SSM_STEPS, WIDTH), BF16),
        ],
        compiler_params=pltpu.CompilerParams(
            dimension_semantics=("parallel",), vmem_limit_bytes=VMEM_LIMIT),
        name="inproj",
    )(x2d, g, w)


ATT_ROWS = 16


def _attn_kernel(lam_ref, g_ref, q_ref, qn_ref, k_ref, v_ref, o_ref, vt_ref, qh_ref, s_ref,
                 p_ref, acc_ref, *, tq, lam_init):
    qi = pl.program_id(2)
    nchunk = tq // ATT_ROWS

    @pl.when(qi == 0)
    def _():
        for c in range(vt_ref.shape[0]):
            vt_ref[c] = v_ref[c * tq:(c + 1) * tq, :].T

    def scaled_t(ref):
        return (ref[...] * (ATT_QK_DIM ** -0.5)).T

    qt = scaled_t(q_ref)
    feat = lax.broadcasted_iota(jnp.int32, qt.shape, 0)
    zero = jnp.zeros_like(qt)
    qh_ref[0] = jnp.where(feat < ATT_QK_DIM, qt, zero)
    qh_ref[1] = jnp.where(feat >= ATT_QK_DIM, qt, zero)
    p_ref[1] = jnp.zeros(p_ref.shape[1:], BF16)
    acc_ref[...] = jnp.zeros_like(acc_ref)

    def scores(h, kb):
        k = k_ref[pl.ds(pl.multiple_of(kb * tq, tq), tq), :]
        s_ref[h] = jnp.dot(k, qh_ref[h], preferred_element_type=F32)

    qlane = lax.broadcasted_iota(jnp.int32, (ATT_ROWS, tq), 1)

    def chunk(h, c, masked):
        x = s_ref[h, c * ATT_ROWS:(c + 1) * ATT_ROWS, :]
        if masked:
            x = jnp.where(qlane >= (c * ATT_ROWS) // CHUNK * CHUNK, x, -jnp.inf)
        return x

    def softmax(h, stats, masked):
        m, l = stats
        mx = chunk(h, 0, masked)
        for c in range(1, nchunk):
            mx = jnp.maximum(mx, chunk(h, c, masked))
        m_new = jnp.maximum(m, jnp.max(mx, axis=0, keepdims=True))
        alpha = jnp.exp(m - m_new)
        mb = jnp.broadcast_to(m_new, (ATT_ROWS, tq))
        tot = None
        for c in range(nchunk):
            p = jnp.exp(chunk(h, c, masked) - mb)
            tot = p if tot is None else tot + p
            p_ref[h, c * ATT_ROWS:(c + 1) * ATT_ROWS, :] = p.astype(BF16)
        l = alpha * l + jnp.sum(tot, axis=0, keepdims=True)
        return alpha, (m_new, l)

    def values(h, kb, alpha):
        acc_ref[h] = alpha * acc_ref[h] + jnp.dot(vt_ref[kb], p_ref[h],
                                                  preferred_element_type=F32)

    def body(kb, carry):
        st0, st1, alpha1 = carry
        values(1, jnp.maximum(kb - 1, 0), alpha1)
        scores(1, kb)
        alpha0, st0 = softmax(0, st0, False)
        values(0, kb, alpha0)
        scores(0, kb + 1)
        alpha1, st1 = softmax(1, st1, False)
        return st0, st1, alpha1

    @pl.when(qi == 0)
    def _():
        scores(0, 0)

    stat0 =(jnp.full((1, tq), -jnp.inf, F32), jnp.zeros((1, tq), F32))
    st0, st1, alpha1 = lax.fori_loop(0, qi, body, (stat0, stat0, jnp.ones((1, tq), F32)))

    values(1, jnp.maximum(qi - 1, 0), alpha1)
    scores(1, qi)
    alpha0, (_, l0) = softmax(0, st0, True)
    values(0, qi, alpha0)
    qh_ref[0] = jnp.where(feat < ATT_QK_DIM, scaled_t(qn_ref), zero)
    scores(0, 0)
    alpha1, (_, l1) = softmax(1, st1, True)
    values(1, qi, alpha1)

    lf = lam_ref[...]
    lam = (jnp.exp(jnp.sum(lf[0:1] * lf[1:2], keepdims=True))
           - jnp.exp(jnp.sum(lf[2:3] * lf[3:4], keepdims=True)) + lam_init)
    o = acc_ref[0] * (1.0 / l0) - acc_ref[1] * (lam / l1)
    scale = lax.rsqrt(jnp.mean(o * o, axis=0, keepdims=True) + 1e-5) * (1.0 - lam_init)
    o_ref[...] = (o * scale * g_ref[...]).T.astype(o_ref.dtype)


def _attention(qkv, lam_qk, subln_g, lam_init, tq):
    _, bsz, _, seq, lanes = qkv.shape
    return pl.pallas_call(
        functools.partial(_attn_kernel, tq=tq, lam_init=lam_init),
        grid=(bsz, ATT_HEADS, seq // tq),
        in_specs=[
            pl.BlockSpec((4, ATT_QK_DIM), lambda b, h, i: (0, 0)),
            pl.BlockSpec((ATT_V_DIM, 1), lambda b, h, i: (0, 0)),
            pl.BlockSpec((None, None, None, tq, lanes), lambda b, h, i: (0, b, h, i, 0)),
            pl.BlockSpec((None, None, None, tq, lanes),
                         lambda b, h, i: (0, b, h, jnp.minimum(i + 1, seq // tq - 1), 0)),
            pl.BlockSpec((None, None, None, seq, lanes), lambda b, h, i: (1, b, h, 0, 0)),
            pl.BlockSpec((None, None, None, seq, lanes), lambda b, h, i: (2, b, h, 0, 0)),
        ],
        out_specs=pl.BlockSpec((None, None, tq, lanes), lambda b, h, i: (b, h, i, 0)),
        out_shape=jax.ShapeDtypeStruct((bsz, ATT_HEADS, seq, lanes), BF16),
        scratch_shapes=[
            pltpu.VMEM((seq // tq, ATT_V_DIM, tq), BF16),
            pltpu.VMEM((2, lanes, tq), BF16),
            pltpu.VMEM((2, tq, tq), F32),
            pltpu.VMEM((2, tq, tq), BF16),
            pltpu.VMEM((2, ATT_V_DIM, tq), F32),
        ],
        compiler_params=pltpu.CompilerParams(
            dimension_semantics=("parallel", "parallel", "arbitrary"),
            vmem_limit_bytes=VMEM_LIMIT),
        name="diff_attention",
    )(lam_qk, subln_g, qkv, qkv, qkv, qkv)


SSM_BLOCKS = 4
SSM_BLOCK_IN = WIDTH // SSM_BLOCKS
SSM_SLAB = SSM_LANES // SSM_BLOCKS


def _ssm_kernel(u_ref, perm_ref, permt_ref, bblk_ref, ar_ref, ai_ref, cblk_ref, d_ref,
                wglu_ref, bglu_ref, o_ref, st_ref, bu_ref, *, bsz, steps):
    @pl.when(pl.program_id(0) == 0)
    def _():
        st_ref[...] = jnp.zeros_like(st_ref)

    slab = SSM_SLAB
    perm = perm_ref[...]
    u = jnp.dot(perm, u_ref[...], preferred_element_type=F32).astype(BF16)
    for j in range(SSM_BLOCKS):
        bu_ref[:, 2 * slab * j:2 * slab * (j + 1)] = jnp.dot(
            u[:, SSM_BLOCK_IN * j:SSM_BLOCK_IN * (j + 1)], bblk_ref[j],
            preferred_element_type=F32)

    for j in range(SSM_BLOCKS):
        re = pl.ds(2 * slab * j, slab)
        im = pl.ds(2 * slab * j + slab, slab)
        ar = jnp.broadcast_to(ar_ref[:, slab * j:slab * (j + 1)], (bsz, slab))
        ai = jnp.broadcast_to(ai_ref[:, slab * j:slab * (j + 1)], (bsz, slab))

        def body(t, carry, re=re, im=im, ar=ar, ai=ai):
            xr, xi = carry
            rows = pl.ds(pl.multiple_of(t * bsz, bsz), bsz)
            nxr = ar * xr - ai * xi + bu_ref[rows, re]
            nxi = ar * xi + ai * xr + bu_ref[rows, im]
            bu_ref[rows, re] = nxr
            bu_ref[rows, im] = nxi
            return nxr, nxi

        xr, xi = lax.fori_loop(0, steps, body, (st_ref[:, re], st_ref[:, im]), unroll=4)
        st_ref[:, re] = xr
        st_ref[:, im] = xi

    y = jnp.concatenate(
        [jnp.dot(bu_ref[:, 2 * slab * j:2 * slab * (j + 1)].astype(BF16), cblk_ref[j],
                 preferred_element_type=F32) for j in range(SSM_BLOCKS)], axis=1)
    y = jax.nn.gelu(y + d_ref[...] * u.astype(F32))
    z = jnp.dot(y.astype(BF16), wglu_ref[...], preferred_element_type=F32) + bglu_ref[...]
    out = (y * jax.nn.sigmoid(z)).astype(BF16)
    o_ref[...] = jnp.dot(permt_ref[...], out, preferred_element_type=F32).astype(o_ref.dtype)


def _ssm(du_rows, bblk, ar, ai, cblk, d_skip, w_glu, b_glu, layer, bsz, seq, steps):
    rows = steps * bsz
    full = lambda c: (0, 0)
    r_out = jnp.arange(rows)[:, None]
    r_in = jnp.arange(rows)[None, :]
    perm = (r_in == (r_out % bsz) * steps + r_out // bsz).astype(BF16)
    return pl.pallas_call(
        functools.partial(_ssm_kernel, bsz=bsz, steps=steps),
        grid=(seq // steps,),
        in_specs=[
            pl.BlockSpec((None, rows, WIDTH), lambda c: (c, 0, 0)),
            pl.BlockSpec((rows, rows), full),
            pl.BlockSpec((rows, rows), full),
            pl.BlockSpec((SSM_BLOCKS, SSM_BLOCK_IN, 2 * SSM_SLAB), lambda c: (0, 0, 0)),
            pl.BlockSpec((1, SSM_LANES), full),
            pl.BlockSpec((1, SSM_LANES), full),
            pl.BlockSpec((SSM_BLOCKS, 2 * SSM_SLAB, SSM_BLOCK_IN), lambda c: (0, 0, 0)),
            pl.BlockSpec((1, WIDTH), full),
            pl.BlockSpec((None, WIDTH, WIDTH), lambda c: (layer, 0, 0)),
            pl.BlockSpec((1, WIDTH), full),
        ],
        out_specs=pl.BlockSpec((None, rows, WIDTH), lambda c: (c, 0, 0)),
        out_shape=jax.ShapeDtypeStruct((seq // steps, rows, WIDTH), BF16),
        scratch_shapes=[pltpu.VMEM((bsz, 2 * SSM_LANES), F32),
                        pltpu.VMEM((rows, 2 * SSM_LANES), F32)],
        compiler_params=pltpu.CompilerParams(
            dimension_semantics=("arbitrary",), vmem_limit_bytes=VMEM_LIMIT),
        name="s5_scan",
    )(du_rows, perm, perm.T, bblk, ar, ai, cblk, d_skip, w_glu, b_glu)


def _ssm_params(a_re, a_im, log_dt, b_re, b_im, c_re, c_im):
    dt = jnp.exp(log_dt)[:, None]
    mag = jnp.exp(dt * a_re)
    ab_re = mag * jnp.cos(dt * a_im)
    ab_im = mag * jnp.sin(dt * a_im)
    den = a_re * a_re + a_im * a_im
    nr, ni = ab_re - 1.0, ab_im
    coef_re = (nr * a_re + ni * a_im) / den
    coef_im = (ni * a_re - nr * a_im) / den
    bb_re = coef_re[..., None] * b_re - coef_im[..., None] * b_im
    bb_im = coef_re[..., None] * b_im + coef_im[..., None] * b_re
    per = SSM_GROUPS // SSM_BLOCKS
    eye = jnp.eye(per, dtype=F32)[None, :, None, :, None]

    def diag(m):
        a, b = m.shape[1:]
        mt = m.reshape(SSM_BLOCKS, per, a, b).transpose(0, 1, 3, 2)
        return (eye * mt[:, :, :, None, :]).reshape(SSM_BLOCKS, per * b, per * a)

    bblk = jnp.concatenate([diag(bb_re), diag(bb_im)], axis=2).astype(BF16)
    cblk = jnp.concatenate([diag(c_re), -diag(c_im)], axis=1).astype(BF16)
    return bblk, ab_re.reshape(1, SSM_LANES), ab_im.reshape(1, SSM_LANES), cblk


def _merge_kernel(gate_ref, ab_ref, ac_ref, ax_ref, hc_ref, hx_ref, bu_ref, bv_ref,
                  yc_ref, yd_ref, x_ref, cw_ref, cb_ref, sgw_ref, sgb_ref, lng_ref, lnb_ref,
                  wbr_ref, wo_ref, o_ref, *, tm):
    s_idx = pl.program_id(1)

    z0 = ac_ref[...].astype(F32) * ax_ref[...].astype(F32)
    halo = hc_ref[...].astype(F32) * hx_ref[...].astype(F32)
    halo = jnp.where(s_idx == 0, jnp.zeros_like(halo), halo)
    hrows = halo.shape[0]
    h1 = halo[hrows - 1:hrows, :]
    h2 = halo[hrows - 2:hrows - 1, :]
    row = lax.broadcasted_iota(jnp.int32, z0.shape, 0)
    z1 = jnp.where(row == 0, h1, pltpu.roll(z0, 1, 0))
    z2 = jnp.where(row == 0, h2, jnp.where(row == 1, h1, pltpu.roll(z0, 2, 0)))
    conv = cw_ref[0:1, :] * z2 + cw_ref[1:2, :] * z1 + cw_ref[2:3, :] * z0 + cb_ref[...]
    y_a = ab_ref[...].astype(F32) * conv

    u = jax.nn.gelu(bu_ref[...].astype(F32))
    v = jax.nn.gelu(bv_ref[...].astype(F32))
    mu = jnp.mean(v, axis=-1, keepdims=True)
    vc = v - mu
    var = jnp.mean(vc * vc, axis=-1, keepdims=True)
    vn = (vc * lax.rsqrt(var + 1e-5) * lng_ref[...] + lnb_ref[...]).astype(BF16)
    tr = lax.broadcasted_iota(jnp.int32, (SG_CHUNK, SG_CHUNK), 0)
    tc = lax.broadcasted_iota(jnp.int32, (SG_CHUNK, SG_CHUNK), 1)
    gd = WIDTH // SG_GROUPS
    cols = []
    for g in range(SG_GROUPS):
        wg = jnp.where(tc <= tr, sgw_ref[g], 0.0).astype(BF16)
        bias = sgb_ref[:, g:g + 1]
        blocks = [jnp.dot(wg, vn[r * SG_CHUNK:(r + 1) * SG_CHUNK, g * gd:(g + 1) * gd],
                          preferred_element_type=F32) + bias
                  for r in range(tm // SG_CHUNK)]
        cols.append(jnp.concatenate(blocks, axis=0))
    y_b = u * jnp.concatenate(cols, axis=1)

    y_c = jnp.concatenate([yc_ref[hd] for hd in range(ATT_HEADS)], axis=1)
    y_d = yd_ref[...].reshape(tm, WIDTH)
    branches = (y_a.astype(BF16), y_b.astype(BF16), y_c, y_d)
    merged = None
    for n, y in enumerate(branches):
        gate = jax.nn.sigmoid(gate_ref[:, n * D_MODEL:(n + 1) * D_MODEL].astype(F32))
        term = gate * jnp.dot(y, wbr_ref[n], preferred_element_type=F32)
        merged = term if merged is None else merged + term
    o_ref[...] = x_ref[...] + jnp.dot(merged.astype(BF16), wo_ref[...],
                                      preferred_element_type=F32)


def _merge(proj3, y_c, y_d, x3, conv_w, conv_b, sg_w, sg_bt, ln_g, ln_b, w_br, w_o, layer, tm):
    bsz, seq, _ = x3.shape
    hrows = 16
    hstep = tm // hrows

    def col(blk):
        return pl.BlockSpec((None, tm, WIDTH), lambda b, s: (b, s, blk))

    def halo(blk):
        return pl.BlockSpec((None, hrows, WIDTH),
                            lambda b, s: (b, jnp.maximum(s * hstep - 1, 0), blk))

    def full(shape):
        return pl.BlockSpec(shape, lambda b, s: (0,) * len(shape))

    return pl.pallas_call(
        functools.partial(_merge_kernel, tm=tm),
        grid=(bsz, seq // tm),
        in_specs=[
            pl.BlockSpec((None, tm, GATE_COLS), lambda b, s: (b, s, 0)),
            col(BLK_AB), col(BLK_AC), col(BLK_AX), halo(BLK_AC), halo(BLK_AX),
            col(BLK_BU), col(BLK_BV),
            pl.BlockSpec((None, ATT_HEADS, tm, ATT_V_DIM), lambda b, s: (b, 0, s, 0)),
            pl.BlockSpec((tm // SSM_STEPS, SSM_STEPS, WIDTH), lambda b, s: (s, b, 0)),
            pl.BlockSpec((None, tm, D_MODEL), lambda b, s: (b, s, 0)),
            full((3, WIDTH)), full((1, WIDTH)),
            full((SG_GROUPS, SG_CHUNK, SG_CHUNK)), full((SG_CHUNK, SG_GROUPS)),
            full((1, WIDTH)), full((1, WIDTH)),
            pl.BlockSpec((None, N_BRANCH, WIDTH, D_MODEL), lambda b, s: (layer, 0, 0, 0)),
            pl.BlockSpec((None, D_MODEL, D_MODEL), lambda b, s: (layer, 0, 0)),
        ],
        out_specs=pl.BlockSpec((None, tm, D_MODEL), lambda b, s: (b, s, 0)),
        out_shape=jax.ShapeDtypeStruct((bsz, seq, D_MODEL), F32),
        compiler_params=pltpu.CompilerParams(
            dimension_semantics=("parallel", "arbitrary"), vmem_limit_bytes=VMEM_LIMIT),
        name="merge",
    )(proj3, proj3, proj3, proj3, proj3, proj3, proj3, proj3, y_c, y_d, x3,
      conv_w, conv_b, sg_w, sg_bt, ln_g, ln_b, w_br, w_o)


FFN_SPLITS = (0, 1536, FFN_HIDDEN)


def _ffn_kernel(x_ref, g_ref, wg_ref, wu_ref, wd_ref, gf_ref, o_ref, *, final):
    x = x_ref[...]
    h = _rms(x, g_ref[...], 1e-6).astype(BF16)
    acc = x
    for lo, hi in zip(FFN_SPLITS[:-1], FFN_SPLITS[1:]):
        cs = slice(lo, hi)
        gate = jnp.dot(h, wg_ref[:, cs], preferred_element_type=F32)
        up = jnp.dot(h, wu_ref[:, cs], preferred_element_type=F32)
        act = (jax.nn.silu(gate) * up).astype(BF16)
        acc = acc + jnp.dot(act, wd_ref[cs, :], preferred_element_type=F32)
    if final:
        acc = _rms(acc, gf_ref[...], 1e-6)
    o_ref[...] = acc


def _ffn(x2d, g, w_gate, w_up, w_down, g_final, layer, final, tm):
    n = x2d.shape[0]
    full = lambda i: (0, 0)
    lay = lambda i: (layer, 0, 0)
    return pl.pallas_call(
        functools.partial(_ffn_kernel, final=final),
        grid=(n // tm,),
        in_specs=[
            pl.BlockSpec((tm, D_MODEL), lambda i: (i, 0)),
            pl.BlockSpec((1, D_MODEL), full),
            pl.BlockSpec((None, D_MODEL, FFN_HIDDEN), lay),
            pl.BlockSpec((None, D_MODEL, FFN_HIDDEN), lay),
            pl.BlockSpec((None, FFN_HIDDEN, D_MODEL), lay),
            pl.BlockSpec((1, D_MODEL), full),
        ],
        out_specs=pl.BlockSpec((tm, D_MODEL), lambda i: (i, 0)),
        out_shape=jax.ShapeDtypeStruct((n, D_MODEL), F32),
        compiler_params=pltpu.CompilerParams(
            dimension_semantics=("parallel",), vmem_limit_bytes=VMEM_LIMIT),
        name="ffn",
    )(x2d, g, w_gate, w_up, w_down, g_final)


def _reorder_in_proj(w):
    return jnp.concatenate([w[..., -GATE_COLS:], w[..., :-GATE_COLS]], axis=-1).astype(BF16)


def kernel(x, g_mix, w_in, conv_w, conv_b, sg_w, sg_b, sg_ln_g, sg_ln_b, lam_qk, subln_g,
           ssm_a_re, ssm_a_im, ssm_log_dt, ssm_b_re, ssm_b_im, ssm_c_re, ssm_c_im, ssm_d,
           w_glu, b_glu, w_br, w_o, g_ffn, w_ffn_gate, w_ffn_up, w_ffn_down, g_final):
    bsz, seq, _ = x.shape
    depth = w_in.shape[0]
    n = bsz * seq
    tm = min(512, seq)
    tq = min(512, seq)
    steps = SSM_STEPS
    row2 = lambda a: a.reshape(1, -1)
    w_in_b = _reorder_in_proj(w_in)
    w_glu_b, w_br_b, w_o_b = w_glu.astype(BF16), w_br.astype(BF16), w_o.astype(BF16)
    w_gate_b, w_up_b, w_down_b = (w.astype(BF16) for w in (w_ffn_gate, w_ffn_up, w_ffn_down))

    for l in range(depth):
        lam_init = 0.8 - 0.6 * math.exp(-0.3 * l)
        proj, qkv, du = _inproj(x.reshape(n, D_MODEL), row2(g_mix[l]), w_in_b, l, bsz, seq, tm)
        proj3 = proj.reshape(bsz, seq, PROJ_COLS)

        y_c = _attention(qkv, lam_qk[l], subln_g[l].reshape(-1, 1), lam_init, tq)

        bblk, ar, ai, cblk = _ssm_params(ssm_a_re[l], ssm_a_im[l], ssm_log_dt[l],
                                         ssm_b_re[l], ssm_b_im[l], ssm_c_re[l], ssm_c_im[l])
        y_d = _ssm(du, bblk, ar, ai, cblk, row2(ssm_d[l]),
                   w_glu_b, row2(b_glu[l]), l, bsz, seq, steps)

        x = _merge(proj3, y_c, y_d, x, conv_w[l], row2(conv_b[l]), sg_w[l], sg_b[l].T,
                   row2(sg_ln_g[l]), row2(sg_ln_b[l]), w_br_b, w_o_b, l, tm)

        x = _ffn(x.reshape(n, D_MODEL), row2(g_ffn[l]), w_gate_b, w_up_b, w_down_b,
                 row2(g_final), l, l == depth - 1, tm).reshape(bsz, seq, D_MODEL)
    return x
```

```python
import functools
import math

import jax
import jax.numpy as jnp
from jax import lax
from jax.experimental import pallas as pl
from jax.experimental.pallas import tpu as pltpu

F32 = jnp.float32
BF16 = jnp.bfloat16

D_MODEL = 1024
WIDTH = 512
N_BRANCH = 4
CHUNK = 64
SG_CHUNK = 128
SG_GROUPS = 4
ATT_HEADS = 4
ATT_QK_DIM = 64
ATT_V_DIM = 128
SSM_GROUP = 16
SSM_GROUPS = 32
SSM_STATE = 64
SSM_LANES = SSM_GROUPS * SSM_STATE
FFN_HIDDEN = 2816

GATE_COLS = N_BRANCH * D_MODEL
GATE_BLOCKS = GATE_COLS // WIDTH
BLK_AB, BLK_AC, BLK_AX, BLK_BU, BLK_BV, BLK_Q, BLK_K, BLK_V, BLK_DU = range(8, 17)
PROJ_BLOCKS = 13
PROJ_COLS = PROJ_BLOCKS * WIDTH
IN_BLOCKS = 17
SSM_STEPS = 64

VMEM_LIMIT = 56 * 1024 * 1024


def _rms(x, g, eps):
    return x * lax.rsqrt(jnp.mean(x * x, axis=-1, keepdims=True) + eps) * g


def _inproj_kernel(x_ref, g_ref, w_ref, o_ref, qkv_ref, du_ref):
    h = _rms(x_ref[...], g_ref[...], 1e-6).astype(BF16)
    for j in range(IN_BLOCKS):
        cols = slice(j * WIDTH, (j + 1) * WIDTH)
        src = (j + IN_BLOCKS - GATE_BLOCKS) % IN_BLOCKS * WIDTH
        acc = jnp.dot(h, w_ref[:, src:src + WIDTH], preferred_element_type=F32).astype(BF16)
        if j == BLK_DU:
            du_ref[...] = acc.reshape(du_ref.shape)
        elif j in (BLK_Q, BLK_K, BLK_V):
            for hd in range(ATT_HEADS):
                qkv_ref[j - BLK_Q, hd] = acc[:, hd * ATT_V_DIM:(hd + 1) * ATT_V_DIM]
        else:
            o_ref[:, cols] = acc


def _inproj(x2d, g, w, layer, bsz, seq, tm):
    n = x2d.shape[0]
    spb = seq // tm
    return pl.pallas_call(
        _inproj_kernel,
        grid=(n // tm,),
        in_specs=[
            pl.BlockSpec((tm, D_MODEL), lambda i: (i, 0)),
            pl.BlockSpec((1, D_MODEL), lambda i: (0, 0)),
            pl.BlockSpec((None, D_MODEL, IN_BLOCKS * WIDTH), lambda i: (layer, 0, 0),
                         pipeline_mode=pl.Buffered(1)),
        ],
        out_specs=[
            pl.BlockSpec((tm, PROJ_COLS), lambda i: (i, 0)),
            pl.BlockSpec((3, None, ATT_HEADS, tm, ATT_V_DIM),
                         lambda i: (0, i // spb, 0, i % spb, 0)),
            pl.BlockSpec((tm // SSM_STEPS, SSM_STEPS, WIDTH), lambda i: (i % spb, i // spb, 0)),
        ],
        out_shape=[
            jax.ShapeDtypeStruct((n, PROJ_COLS), BF16),
            jax.ShapeDtypeStruct((3, bsz, ATT_HEADS, seq, ATT_V_DIM), BF16),
            jax.ShapeDtypeStruct((seq // SSM_STEPS, bsz * SSM_STEPS, WIDTH), BF16),
        ],
        compiler_params=pltpu.CompilerParams(
            dimension_semantics=("parallel",), vmem_limit_bytes=VMEM_LIMIT),
        name="inproj",
    )(x2d, g, w)


ATT_ROWS = 16


def _attn_kernel(lam_ref, g_ref, q_ref, k_ref, v_ref, o_ref, vt_ref, qh_ref, s_ref, p_ref,
                 acc_ref, *, tq, lam_init):
    nq = q_ref.shape[0] // tq
    nchunk = tq // ATT_ROWS

    for c in range(nq):
        vt_ref[c] = v_ref[c * tq:(c + 1) * tq, :].T

    lf = lam_ref[...]
    lam = (jnp.exp(jnp.sum(lf[0:1] * lf[1:2], keepdims=True))
           - jnp.exp(jnp.sum(lf[2:3] * lf[3:4], keepdims=True)) + lam_init)

    def rows(i):
        return pl.ds(pl.multiple_of(i * tq, tq), tq)

    feat = lax.broadcasted_iota(jnp.int32, (ATT_V_DIM, tq), 0)

    def set_q(h, qi):
        qt = (q_ref[rows(qi), :] * (ATT_QK_DIM ** -0.5)).T
        keep = feat < ATT_QK_DIM if h == 0 else feat >= ATT_QK_DIM
        qh_ref[h] = jnp.where(keep, qt, jnp.zeros_like(qt))

    def scores(h, kb):
        s_ref[h] = jnp.dot(k_ref[rows(kb), :], qh_ref[h], preferred_element_type=F32)

    qlane = lax.broadcasted_iota(jnp.int32, (ATT_ROWS, tq), 1)

    def chunk(h, c, masked):
        x = s_ref[h, c * ATT_ROWS:(c + 1) * ATT_ROWS, :]
        if masked:
            x = jnp.where(qlane >= (c * ATT_ROWS) // CHUNK * CHUNK, x, -jnp.inf)
        return x

    def softmax(h, stats, masked):
        m, l = stats
        mx = chunk(h, 0, masked)
        for c in range(1, nchunk):
            mx = jnp.maximum(mx, chunk(h, c, masked))
        m_new = jnp.maximum(m, jnp.max(mx, axis=0, keepdims=True))
        alpha = jnp.exp(m - m_new)
        mb = jnp.broadcast_to(m_new, (ATT_ROWS, tq))
        tot = None
        for c in range(nchunk):
            p = jnp.exp(chunk(h, c, masked) - mb)
            tot = p if tot is None else tot + p
            p_ref[h, c * ATT_ROWS:(c + 1) * ATT_ROWS, :] = p.astype(BF16)
        l = alpha * l + jnp.sum(tot, axis=0, keepdims=True)
        return alpha, (m_new, l)

    def values(h, kb, alpha):
        acc_ref[h] = alpha * acc_ref[h] + jnp.dot(vt_ref[kb], p_ref[h],
                                                  preferred_element_type=F32)

    def body(kb, carry):
        st0, st1, alpha1 = carry
        values(1, jnp.maximum(kb - 1, 0), alpha1)
        scores(1, kb)
        alpha0, st0 = softmax(0, st0, False)
        values(0, kb, alpha0)
        scores(0, kb + 1)
        alpha1, st1 = softmax(1, st1, False)
        return st0, st1, alpha1

    def tile(qi, _):
        set_q(1, qi)
        p_ref[1] = jnp.zeros(p_ref.shape[1:], BF16)
        acc_ref[...] = jnp.zeros_like(acc_ref)
        stat0 = (jnp.full((1, tq), -jnp.inf, F32), jnp.zeros((1, tq), F32))
        st0, st1, alpha1 = lax.fori_loop(0, qi, body, (stat0, stat0, jnp.ones((1, tq), F32)))

        values(1, jnp.maximum(qi - 1, 0), alpha1)
        scores(1, qi)
        alpha0, (_, l0) = softmax(0, st0, True)
        values(0, qi, alpha0)
        set_q(0, jnp.minimum(qi + 1, nq - 1))
        scores(0, 0)
        alpha1, (_, l1) = softmax(1, st1, True)
        values(1, qi, alpha1)

        o = acc_ref[0] * (1.0 / l0) - acc_ref[1] * (lam / l1)
        scale = lax.rsqrt(jnp.mean(o * o, axis=0, keepdims=True) + 1e-5) * (1.0 - lam_init)
        o_ref[rows(qi), :] = (o * scale * g_ref[...]).T.astype(o_ref.dtype)
        return 0

    set_q(0, 0)
    scores(0, 0)
    lax.fori_loop(0, nq, tile, 0)


def _attention(qkv, lam_qk, subln_g, lam_init, tq):
    _, bsz, _, seq, lanes = qkv.shape
    return pl.pallas_call(
        functools.partial(_attn_kernel, tq=tq, lam_init=lam_init),
        grid=(bsz, ATT_HEADS),
        in_specs=[
            pl.BlockSpec((4, ATT_QK_DIM), lambda b, h: (0, 0)),
            pl.BlockSpec((ATT_V_DIM, 1), lambda b, h: (0, 0)),
            pl.BlockSpec((None, None, None, seq, lanes), lambda b, h: (0, b, h, 0, 0)),
            pl.BlockSpec((None, None, None, seq, lanes), lambda b, h: (1, b, h, 0, 0)),
            pl.BlockSpec((None, None, None, seq, lanes), lambda b, h: (2, b, h, 0, 0)),
        ],
        out_specs=pl.BlockSpec((None, None, seq, lanes), lambda b, h: (b, h, 0, 0)),
        out_shape=jax.ShapeDtypeStruct((bsz, ATT_HEADS, seq, lanes), BF16),
        scratch_shapes=[
            pltpu.VMEM((seq // tq, ATT_V_DIM, tq), BF16),
            pltpu.VMEM((2, lanes, tq), BF16),
            pltpu.VMEM((2, tq, tq), F32),
            pltpu.VMEM((2, tq, tq), BF16),
            pltpu.VMEM((2, ATT_V_DIM, tq), F32),
        ],
        compiler_params=pltpu.CompilerParams(
            dimension_semantics=("parallel", "parallel"), vmem_limit_bytes=VMEM_LIMIT),
        name="diff_attention",
    )(lam_qk, subln_g, qkv, qkv, qkv)


SSM_BLOCKS = 4
SSM_BLOCK_IN = WIDTH // SSM_BLOCKS
SSM_SLAB = SSM_LANES // SSM_BLOCKS


def _ssm_kernel(u_ref, perm_ref, permt_ref, bblk_ref, ar_ref, ai_ref, cblk_ref, d_ref,
                wglu_ref, bglu_ref, o_ref, st_ref, bu_ref, *, bsz, steps):
    @pl.when(pl.program_id(0) == 0)
    def _():
        st_ref[...] = jnp.zeros_like(st_ref)

    slab = SSM_SLAB
    perm = perm_ref[...]
    u = jnp.dot(perm, u_ref[...], preferred_element_type=F32).astype(BF16)
    for j in range(SSM_BLOCKS):
        bu_ref[:, 2 * slab * j:2 * slab * (j + 1)] = jnp.dot(
            u[:, SSM_BLOCK_IN * j:SSM_BLOCK_IN * (j + 1)], bblk_ref[j],
            preferred_element_type=F32)

    for j in range(SSM_BLOCKS):
        re = pl.ds(2 * slab * j, slab)
        im = pl.ds(2 * slab * j + slab, slab)
        ar = jnp.broadcast_to(ar_ref[:, slab * j:slab * (j + 1)], (bsz, slab))
        ai = jnp.broadcast_to(ai_ref[:, slab * j:slab * (j + 1)], (bsz, slab))

        def body(t, carry, re=re, im=im, ar=ar, ai=ai):
            xr, xi = carry
            rows = pl.ds(pl.multiple_of(t * bsz, bsz), bsz)
            nxr = ar * xr - ai * xi + bu_ref[rows, re]
            nxi = ar * xi + ai * xr + bu_ref[rows, im]
            bu_ref[rows, re] = nxr
            bu_ref[rows, im] = nxi
            return nxr, nxi

        xr, xi = lax.fori_loop(0, steps, body, (st_ref[:, re], st_ref[:, im]), unroll=4)
        st_ref[:, re] = xr
        st_ref[:, im] = xi

    y = jnp.concatenate(
        [jnp.dot(bu_ref[:, 2 * slab * j:2 * slab * (j + 1)].astype(BF16), cblk_ref[j],
                 preferred_element_type=F32) for j in range(SSM_BLOCKS)], axis=1)
    y = jax.nn.gelu(y + d_ref[...] * u.astype(F32))
    z = jnp.dot(y.astype(BF16), wglu_ref[...], preferred_element_type=F32) + bglu_ref[...]
    out = (y * jax.nn.sigmoid(z)).astype(BF16)
    o_ref[...] = jnp.dot(permt_ref[...], out, preferred_element_type=F32).astype(o_ref.dtype)


def _ssm(du_rows, bblk, ar, ai, cblk, d_skip, w_glu, b_glu, layer, bsz, seq, steps):
    rows = steps * bsz
    full = lambda c: (0, 0)
    r_out = jnp.arange(rows)[:, None]
    r_in = jnp.arange(rows)[None, :]
    perm = (r_in == (r_out % bsz) * steps + r_out // bsz).astype(BF16)
    return pl.pallas_call(
        functools.partial(_ssm_kernel, bsz=bsz, steps=steps),
        grid=(seq // steps,),
        in_specs=[
            pl.BlockSpec((None, rows, WIDTH), lambda c: (c, 0, 0)),
            pl.BlockSpec((rows, rows), full),
            pl.BlockSpec((rows, rows), full),
            pl.BlockSpec((SSM_BLOCKS, SSM_BLOCK_IN, 2 * SSM_SLAB), lambda c: (0, 0, 0)),
            pl.BlockSpec((1, SSM_LANES), full),
            pl.BlockSpec((1, SSM_LANES), full),
            pl.BlockSpec((SSM_BLOCKS, 2 * SSM_SLAB, SSM_BLOCK_IN), lambda c: (0, 0, 0)),
            pl.BlockSpec((1, WIDTH), full),
            pl.BlockSpec((None, WIDTH, WIDTH), lambda c: (layer, 0, 0)),
            pl.BlockSpec((1, WIDTH), full),
        ],
        out_specs=pl.BlockSpec((None, rows, WIDTH), lambda c: (c, 0, 0)),
        out_shape=jax.ShapeDtypeStruct((seq // steps, rows, WIDTH), BF16),
        scratch_shapes=[pltpu.VMEM((bsz, 2 * SSM_LANES), F32),
                        pltpu.VMEM((rows, 2 * SSM_LANES), F32)],
        compiler_params=pltpu.CompilerParams(
            dimension_semantics=("arbitrary",), vmem_limit_bytes=VMEM_LIMIT),
        name="s5_scan",
    )(du_rows, perm, perm.T, bblk, ar, ai, cblk, d_skip, w_glu, b_glu)


def _ssm_params(a_re, a_im, log_dt, b_re, b_im, c_re, c_im):
    dt = jnp.exp(log_dt)[:, None]
    mag = jnp.exp(dt * a_re)
    ab_re = mag * jnp.cos(dt * a_im)
    ab_im = mag * jnp.sin(dt * a_im)
    den = a_re * a_re + a_im * a_im
    nr, ni = ab_re - 1.0, ab_im
    coef_re = (nr * a_re + ni * a_im) / den
    coef_im = (ni * a_re - nr * a_im) / den
    bb_re = coef_re[..., None] * b_re - coef_im[..., None] * b_im
    bb_im = coef_re[..., None] * b_im + coef_im[..., None] * b_re
    per = SSM_GROUPS // SSM_BLOCKS
    eye = jnp.eye(per, dtype=F32)[None, :, None, :, None]

    def diag(m):
        a, b = m.shape[1:]
        mt = m.reshape(SSM_BLOCKS, per, a, b).transpose(0, 1, 3, 2)
        return (eye * mt[:, :, :, None, :]).reshape(SSM_BLOCKS, per * b, per * a)

    bblk = jnp.concatenate([diag(bb_re), diag(bb_im)], axis=2).astype(BF16)
    cblk = jnp.concatenate([diag(c_re), -diag(c_im)], axis=1).astype(BF16)
    return bblk, ab_re.reshape(1, SSM_LANES), ab_im.reshape(1, SSM_LANES), cblk


def _merge_kernel(gate_ref, ab_ref, ac_ref, ax_ref, hc_ref, hx_ref, bu_ref, bv_ref,
                  yc_ref, yd_ref, x_ref, cw_ref, cb_ref, sgw_ref, sgb_ref, lng_ref, lnb_ref,
                  wbr_ref, wo_ref, o_ref, *, tm):
    s_idx = pl.program_id(1)

    z0 = ac_ref[...].astype(F32) * ax_ref[...].astype(F32)
    halo = hc_ref[...].astype(F32) * hx_ref[...].astype(F32)
    halo = jnp.where(s_idx == 0, jnp.zeros_like(halo), halo)
    hrows = halo.shape[0]
    h1 = halo[hrows - 1:hrows, :]
    h2 = halo[hrows - 2:hrows - 1, :]
    row = lax.broadcasted_iota(jnp.int32, z0.shape, 0)
    z1 = jnp.where(row == 0, h1, pltpu.roll(z0, 1, 0))
    z2 = jnp.where(row == 0, h2, jnp.where(row == 1, h1, pltpu.roll(z0, 2, 0)))
    conv = cw_ref[0:1, :] * z2 + cw_ref[1:2, :] * z1 + cw_ref[2:3, :] * z0 + cb_ref[...]
    y_a = ab_ref[...].astype(F32) * conv

    u = jax.nn.gelu(bu_ref[...].astype(F32))
    v = jax.nn.gelu(bv_ref[...].astype(F32))
    mu = jnp.mean(v, axis=-1, keepdims=True)
    vc = v - mu
    var = jnp.mean(vc * vc, axis=-1, keepdims=True)
    vn = (vc * lax.rsqrt(var + 1e-5) * lng_ref[...] + lnb_ref[...]).astype(BF16)
    tr = lax.broadcasted_iota(jnp.int32, (SG_CHUNK, SG_CHUNK), 0)
    tc = lax.broadcasted_iota(jnp.int32, (SG_CHUNK, SG_CHUNK), 1)
    gd = WIDTH // SG_GROUPS
    cols = []
    for g in range(SG_GROUPS):
        wg = jnp.where(tc <= tr, sgw_ref[g], 0.0).astype(BF16)
        bias = sgb_ref[:, g:g + 1]
        blocks = [jnp.dot(wg, vn[r * SG_CHUNK:(r + 1) * SG_CHUNK, g * gd:(g + 1) * gd],
                          preferred_element_type=F32) + bias
                  for r in range(tm // SG_CHUNK)]
        cols.append(jnp.concatenate(blocks, axis=0))
    y_b = u * jnp.concatenate(cols, axis=1)

    y_c = jnp.concatenate([yc_ref[hd] for hd in range(ATT_HEADS)], axis=1)
    y_d = yd_ref[...].reshape(tm, WIDTH)
    branches = (y_a.astype(BF16), y_b.astype(BF16), y_c, y_d)
    merged = None
    for n, y in enumerate(branches):
        gate = jax.nn.sigmoid(gate_ref[:, n * D_MODEL:(n + 1) * D_MODEL].astype(F32))
        term = gate * jnp.dot(y, wbr_ref[n], preferred_element_type=F32)
        merged = term if merged is None else merged + term
    o_ref[...] = x_ref[...] + jnp.dot(merged.astype(BF16), wo_ref[...],
                                      preferred_element_type=F32)


def _merge(proj3, y_c, y_d, x3, conv_w, conv_b, sg_w, sg_bt, ln_g, ln_b, w_br, w_o, layer, tm):
    bsz, seq, _ = x3.shape
    hrows = 16
    hstep = tm // hrows

    def col(blk):
        return pl.BlockSpec((None, tm, WIDTH), lambda b, s: (b, s, blk))

    def halo(blk):
        return pl.BlockSpec((None, hrows, WIDTH),
                            lambda b, s: (b, jnp.maximum(s * hstep - 1, 0), blk))

    def full(shape):
        return pl.BlockSpec(shape, lambda b, s: (0,) * len(shape))

    return pl.pallas_call(
        functools.partial(_merge_kernel, tm=tm),
        grid=(bsz, seq // tm),
        in_specs=[
            pl.BlockSpec((None, tm, GATE_COLS), lambda b, s: (b, s, 0)),
            col(BLK_AB), col(BLK_AC), col(BLK_AX), halo(BLK_AC), halo(BLK_AX),
            col(BLK_BU), col(BLK_BV),
            pl.BlockSpec((None, ATT_HEADS, tm, ATT_V_DIM), lambda b, s: (b, 0, s, 0)),
            pl.BlockSpec((tm // SSM_STEPS, SSM_STEPS, WIDTH), lambda b, s: (s, b, 0)),
            pl.BlockSpec((None, tm, D_MODEL), lambda b, s: (b, s, 0)),
            full((3, WIDTH)), full((1, WIDTH)),
            full((SG_GROUPS, SG_CHUNK, SG_CHUNK)), full((SG_CHUNK, SG_GROUPS)),
            full((1, WIDTH)), full((1, WIDTH)),
            pl.BlockSpec((None, N_BRANCH, WIDTH, D_MODEL), lambda b, s: (layer, 0, 0, 0)),
            pl.BlockSpec((None, D_MODEL, D_MODEL), lambda b, s: (layer, 0, 0)),
        ],
        out_specs=pl.BlockSpec((None, tm, D_MODEL), lambda b, s: (b, s, 0)),
        out_shape=jax.ShapeDtypeStruct((bsz, seq, D_MODEL), F32),
        compiler_params=pltpu.CompilerParams(
            dimension_semantics=("parallel", "arbitrary"), vmem_limit_bytes=VMEM_LIMIT),
        name="merge",
    )(proj3, proj3, proj3, proj3, proj3, proj3, proj3, proj3, y_c, y_d, x3,
      conv_w, conv_b, sg_w, sg_bt, ln_g, ln_b, w_br, w_o)


FFN_SPLITS = (0, 1536, FFN_HIDDEN)


def _ffn_kernel(x_ref, g_ref, wg_ref, wu_ref, wd_ref, gf_ref, o_ref, *, final):
    x = x_ref[...]
    h = _rms(x, g_ref[...], 1e-6).astype(BF16)
    acc = x
    for lo, hi in zip(FFN_SPLITS[:-1], FFN_SPLITS[1:]):
        cs = slice(lo, hi)
        gate = jnp.dot(h, wg_ref[:, cs], preferred_element_type=F32)
        up = jnp.dot(h, wu_ref[:, cs], preferred_element_type=F32)
        act = (jax.nn.silu(gate) * up).astype(BF16)
        acc = acc + jnp.dot(act, wd_ref[cs, :], preferred_element_type=F32)
    if final:
        acc = _rms(acc, gf_ref[...], 1e-6)
    o_ref[...] = acc


def _ffn(x2d, g, w_gate, w_up, w_down, g_final, layer, final, tm):
    n = x2d.shape[0]
    full = lambda i: (0, 0)
    lay = lambda i: (layer, 0, 0)
    return pl.pallas_call(
        functools.partial(_ffn_kernel, final=final),
        grid=(n // tm,),
        in_specs=[
            pl.BlockSpec((tm, D_MODEL), lambda i: (i, 0)),
            pl.BlockSpec((1, D_MODEL), full),
            pl.BlockSpec((None, D_MODEL, FFN_HIDDEN), lay),
            pl.BlockSpec((None, D_MODEL, FFN_HIDDEN), lay),
            pl.BlockSpec((None, FFN_HIDDEN, D_MODEL), lay),
            pl.BlockSpec((1, D_MODEL), full),
        ],
        out_specs=pl.BlockSpec((tm, D_MODEL), lambda i: (i, 0)),
        out_shape=jax.ShapeDtypeStruct((n, D_MODEL), F32),
        compiler_params=pltpu.CompilerParams(
            dimension_semantics=("parallel",), vmem_limit_bytes=VMEM_LIMIT),
        name="ffn",
    )(x2d, g, w_gate, w_up, w_down, g_final)


def kernel(x, g_mix, w_in, conv_w, conv_b, sg_w, sg_b, sg_ln_g, sg_ln_b, lam_qk, subln_g,
           ssm_a_re, ssm_a_im, ssm_log_dt, ssm_b_re, ssm_b_im, ssm_c_re, ssm_c_im, ssm_d,
           w_glu, b_glu, w_br, w_o, g_ffn, w_ffn_gate, w_ffn_up, w_ffn_down, g_final):
    bsz, seq, _ = x.shape
    depth = w_in.shape[0]
    n = bsz * seq
    tm = min(512, seq)
    tq = min(512, seq)
    steps = SSM_STEPS
    row2 = lambda a: a.reshape(1, -1)
    w_in_b, w_glu_b, w_br_b, w_o_b = (w.astype(BF16) for w in (w_in, w_glu, w_br, w_o))
    w_gate_b, w_up_b, w_down_b = (w.astype(BF16) for w in (w_ffn_gate, w_ffn_up, w_ffn_down))

    for l in range(depth):
        lam_init = 0.8 - 0.6 * math.exp(-0.3 * l)
        proj, qkv, du = _inproj(x.reshape(n, D_MODEL), row2(g_mix[l]), w_in_b, l, bsz, seq, tm)
        proj3 = proj.reshape(bsz, seq, PROJ_COLS)

        y_c = _attention(qkv, lam_qk[l], subln_g[l].reshape(-1, 1), lam_init, tq)

        bblk, ar, ai, cblk = _ssm_params(ssm_a_re[l], ssm_a_im[l], ssm_log_dt[l],
                                         ssm_b_re[l], ssm_b_im[l], ssm_c_re[l], ssm_c_im[l])
        y_d = _ssm(du, bblk, ar, ai, cblk, row2(ssm_d[l]),
                   w_glu_b, row2(b_glu[l]), l, bsz, seq, steps)

        x = _merge(proj3, y_c, y_d, x, conv_w[l], row2(conv_b[l]), sg_w[l], sg_b[l].T,
                   row2(sg_ln_g[l]), row2(sg_ln_b[l]), w_br_b, w_o_b, l, tm)

        x = _ffn(x.reshape(n, D_MODEL), row2(g_ffn[l]), w_gate_b, w_up_b, w_down_b,
                 row2(g_final), l, l == depth - 1, tm).reshape(bsz, seq, D_MODEL)
    return x
```

```python
import functools
import math

import jax
import jax.numpy as jnp
from jax import lax
from jax.experimental import pallas as pl
from jax.experimental.pallas import tpu as pltpu

F32 = jnp.float32
BF16 = jnp.bfloat16

D_MODEL = 1024
WIDTH = 512
N_BRANCH = 4
CHUNK = 64
SG_CHUNK = 128
SG_GROUPS = 4
ATT_HEADS = 4
ATT_QK_DIM = 64
ATT_V_DIM = 128
SSM_GROUP = 16
SSM_GROUPS = 32
SSM_STATE = 64
SSM_LANES = SSM_GROUPS * SSM_STATE
FFN_HIDDEN = 2816

GATE_COLS = N_BRANCH * D_MODEL
GATE_BLOCKS = GATE_COLS // WIDTH
BLK_AB, BLK_AC, BLK_AX, BLK_BU, BLK_BV, BLK_Q, BLK_K, BLK_V, BLK_DU = range(8, 17)
PROJ_BLOCKS = 13
PROJ_COLS = PROJ_BLOCKS * WIDTH
IN_BLOCKS = 17
SSM_STEPS = 64

VMEM_LIMIT = 56 * 1024 * 1024


def _rms(x, g, eps):
    return x * lax.rsqrt(jnp.mean(x * x, axis=-1, keepdims=True) + eps) * g


def _inproj_kernel(x_ref, g_ref, w_ref, o_ref, qkv_ref, du_ref):
    h = _rms(x_ref[...], g_ref[...], 1e-6).astype(BF16)
    for j in range(IN_BLOCKS):
        cols = slice(j * WIDTH, (j + 1) * WIDTH)
        src = (j + IN_BLOCKS - GATE_BLOCKS) % IN_BLOCKS * WIDTH
        acc = jnp.dot(h, w_ref[:, src:src + WIDTH], preferred_element_type=F32).astype(BF16)
        if j == BLK_DU:
            du_ref[...] = acc.reshape(du_ref.shape)
        elif j in (BLK_Q, BLK_K, BLK_V):
            for hd in range(ATT_HEADS):
                qkv_ref[j - BLK_Q, hd] = acc[:, hd * ATT_V_DIM:(hd + 1) * ATT_V_DIM]
        else:
            o_ref[:, cols] = acc


def _inproj(x2d, g, w, layer, bsz, seq, tm):
    n = x2d.shape[0]
    spb = seq // tm
    return pl.pallas_call(
        _inproj_kernel,
        grid=(n // tm,),
        in_specs=[
            pl.BlockSpec((tm, D_MODEL), lambda i: (i, 0)),
            pl.BlockSpec((1, D_MODEL), lambda i: (0, 0)),
            pl.BlockSpec((None, D_MODEL, IN_BLOCKS * WIDTH), lambda i: (layer, 0, 0),
                         pipeline_mode=pl.Buffered(1)),
        ],
        out_specs=[
            pl.BlockSpec((tm, PROJ_COLS), lambda i: (i, 0)),
            pl.BlockSpec((3, None, ATT_HEADS, tm, ATT_V_DIM),
                         lambda i: (0, i // spb, 0, i % spb, 0)),
            pl.BlockSpec((tm // SSM_STEPS, SSM_STEPS, WIDTH), lambda i: (i % spb, i // spb, 0)),
        ],
        out_shape=[
            jax.ShapeDtypeStruct((n, PROJ_COLS), BF16),
            jax.ShapeDtypeStruct((3, bsz, ATT_HEADS, seq, ATT_V_DIM), BF16),
            jax.ShapeDtypeStruct((seq // SSM_STEPS, bsz * SSM_STEPS, WIDTH), BF16),
        ],
        compiler_params=pltpu.CompilerParams(
            dimension_semantics=("parallel",), vmem_limit_bytes=VMEM_LIMIT),
        name="inproj",
    )(x2d, g, w)


ATT_ROWS = 16


def _attn_kernel(lam_ref, g_ref, q_ref, k_ref, v_ref, o_ref, vt_ref, qh_ref, s_ref, p_ref,
                 acc_ref, *, tq, lam_init):
    nq = q_ref.shape[0] // tq
    nchunk = tq // ATT_ROWS

    for c in range(nq):
        vt_ref[c] = v_ref[c * tq:(c + 1) * tq, :].T

    lf = lam_ref[...]
    lam = (jnp.exp(jnp.sum(lf[0:1] * lf[1:2], keepdims=True))
           - jnp.exp(jnp.sum(lf[2:3] * lf[3:4], keepdims=True)) + lam_init)

    def rows(i):
        return pl.ds(pl.multiple_of(i * tq, tq), tq)

    feat = lax.broadcasted_iota(jnp.int32, (ATT_V_DIM, tq), 0)

    def set_q(h, qi):
        qt = (q_ref[rows(qi), :] * (ATT_QK_DIM ** -0.5)).T
        keep = feat < ATT_QK_DIM if h == 0 else feat >= ATT_QK_DIM
        qh_ref[h] = jnp.where(keep, qt, jnp.zeros_like(qt))

    def scores(h, kb):
        s_ref[h] = jnp.dot(k_ref[rows(kb), :], qh_ref[h], preferred_element_type=F32)

    qlane = lax.broadcasted_iota(jnp.int32, (ATT_ROWS, tq), 1)

    def chunk(h, c, masked):
        x = s_ref[h, c * ATT_ROWS:(c + 1) * ATT_ROWS, :]
        if masked:
            x = jnp.where(qlane >= (c * ATT_ROWS) // CHUNK * CHUNK, x, -jnp.inf)
        return x

    def softmax(h, stats, masked):
        m, l = stats
        mx = chunk(h, 0, masked)
        for c in range(1, nchunk):
            mx = jnp.maximum(mx, chunk(h, c, masked))
        m_new = jnp.maximum(m, jnp.max(mx, axis=0, keepdims=True))
        alpha = jnp.exp(m - m_new)
        mb = jnp.broadcast_to(m_new, (ATT_ROWS, tq))
        tot = None
        for c in range(nchunk):
            p = jnp.exp((chunk(h, c, masked) - mb).astype(BF16))
            tot = p.astype(F32) if tot is None else tot + p.astype(F32)
            p_ref[h, c * ATT_ROWS:(c + 1) * ATT_ROWS, :] = p
        l = alpha * l + jnp.sum(tot, axis=0, keepdims=True)
        return alpha, (m_new, l)

    def values(h, kb, alpha):
        acc_ref[h] = alpha * acc_ref[h] + jnp.dot(vt_ref[kb], p_ref[h],
                                                  preferred_element_type=F32)

    def body(kb, carry):
        st0, st1, alpha1 = carry
        values(1, jnp.maximum(kb - 1, 0), alpha1)
        scores(1, kb)
        alpha0, st0 = softmax(0, st0, False)
        values(0, kb, alpha0)
        scores(0, kb + 1)
        alpha1, st1 = softmax(1, st1, False)
        return st0, st1, alpha1

    def tile(qi, _):
        set_q(1, qi)
        p_ref[1] = jnp.zeros(p_ref.shape[1:], BF16)
        acc_ref[...] = jnp.zeros_like(acc_ref)
        stat0 = (jnp.full((1, tq), -jnp.inf, F32), jnp.zeros((1, tq), F32))
        st0, st1, alpha1 = lax.fori_loop(0, qi, body, (stat0, stat0, jnp.ones((1, tq), F32)))

        values(1, jnp.maximum(qi - 1, 0), alpha1)
        scores(1, qi)
        alpha0, (_, l0) = softmax(0, st0, True)
        values(0, qi, alpha0)
        set_q(0, jnp.minimum(qi + 1, nq - 1))
        scores(0, 0)
        alpha1, (_, l1) = softmax(1, st1, True)
        values(1, qi, alpha1)

        o = acc_ref[0] * (1.0 / l0) - acc_ref[1] * (lam / l1)
        scale = lax.rsqrt(jnp.mean(o * o, axis=0, keepdims=True) + 1e-5) * (1.0 - lam_init)
        o_ref[rows(qi), :] = (o * scale * g_ref[...]).T.astype(o_ref.dtype)
        return 0

    set_q(0, 0)
    scores(0, 0)
    lax.fori_loop(0, nq, tile, 0)


def _attention(qkv, lam_qk, subln_g, lam_init, tq):
    _, bsz, _, seq, lanes = qkv.shape
    return pl.pallas_call(
        functools.partial(_attn_kernel, tq=tq, lam_init=lam_init),
        grid=(bsz, ATT_HEADS),
        in_specs=[
            pl.BlockSpec((4, ATT_QK_DIM), lambda b, h: (0, 0)),
            pl.BlockSpec((ATT_V_DIM, 1), lambda b, h: (0, 0)),
            pl.BlockSpec((None, None, None, seq, lanes), lambda b, h: (0, b, h, 0, 0)),
            pl.BlockSpec((None, None, None, seq, lanes), lambda b, h: (1, b, h, 0, 0)),
            pl.BlockSpec((None, None, None, seq, lanes), lambda b, h: (2, b, h, 0, 0)),
        ],
        out_specs=pl.BlockSpec((None, None, seq, lanes), lambda b, h: (b, h, 0, 0)),
        out_shape=jax.ShapeDtypeStruct((bsz, ATT_HEADS, seq, lanes), BF16),
        scratch_shapes=[
            pltpu.VMEM((seq // tq, ATT_V_DIM, tq), BF16),
            pltpu.VMEM((2, lanes, tq), BF16),
            pltpu.VMEM((2, tq, tq), F32),
            pltpu.VMEM((2, tq, tq), BF16),
            pltpu.VMEM((2, ATT_V_DIM, tq), F32),
        ],
        compiler_params=pltpu.CompilerParams(
            dimension_semantics=("parallel", "parallel"), vmem_limit_bytes=VMEM_LIMIT),
        name="diff_attention",
    )(lam_qk, subln_g, qkv, qkv, qkv)


SSM_BLOCKS = 4
SSM_BLOCK_IN = WIDTH // SSM_BLOCKS
SSM_SLAB = SSM_LANES // SSM_BLOCKS


def _ssm_kernel(u_ref, perm_ref, permt_ref, bblk_ref, ar_ref, ai_ref, cblk_ref, d_ref,
                wglu_ref, bglu_ref, o_ref, st_ref, bu_ref, *, bsz, steps):
    @pl.when(pl.program_id(0) == 0)
    def _():
        st_ref[...] = jnp.zeros_like(st_ref)

    slab = SSM_SLAB
    perm = perm_ref[...]
    u = jnp.dot(perm, u_ref[...], preferred_element_type=F32).astype(BF16)
    for j in range(SSM_BLOCKS):
        bu_ref[:, 2 * slab * j:2 * slab * (j + 1)] = jnp.dot(
            u[:, SSM_BLOCK_IN * j:SSM_BLOCK_IN * (j + 1)], bblk_ref[j],
            preferred_element_type=F32)

    for j in range(SSM_BLOCKS):
        re = pl.ds(2 * slab * j, slab)
        im = pl.ds(2 * slab * j + slab, slab)
        ar = jnp.broadcast_to(ar_ref[:, slab * j:slab * (j + 1)], (bsz, slab))
        ai = jnp.broadcast_to(ai_ref[:, slab * j:slab * (j + 1)], (bsz, slab))

        def body(t, carry, re=re, im=im, ar=ar, ai=ai):
            xr, xi = carry
            rows = pl.ds(pl.multiple_of(t * bsz, bsz), bsz)
            nxr = ar * xr - ai * xi + bu_ref[rows, re]
            nxi = ar * xi + ai * xr + bu_ref[rows, im]
            bu_ref[rows, re] = nxr
            bu_ref[rows, im] = nxi
            return nxr, nxi

        xr, xi = lax.fori_loop(0, steps, body, (st_ref[:, re], st_ref[:, im]), unroll=4)
        st_ref[:, re] = xr
        st_ref[:, im] = xi

    y = jnp.concatenate(
        [jnp.dot(bu_ref[:, 2 * slab * j:2 * slab * (j + 1)].astype(BF16), cblk_ref[j],
                 preferred_element_type=F32) for j in range(SSM_BLOCKS)], axis=1)
    y = jax.nn.gelu(y + d_ref[...] * u.astype(F32))
    z = jnp.dot(y.astype(BF16), wglu_ref[...], preferred_element_type=F32) + bglu_ref[...]
    out = (y * jax.nn.sigmoid(z)).astype(BF16)
    o_ref[...] = jnp.dot(permt_ref[...], out, preferred_element_type=F32).astype(o_ref.dtype)


def _ssm(du_rows, bblk, ar, ai, cblk, d_skip, w_glu, b_glu, layer, bsz, seq, steps):
    rows = steps * bsz
    full = lambda c: (0, 0)
    r_out = jnp.arange(rows)[:, None]
    r_in = jnp.arange(rows)[None, :]
    perm = (r_in == (r_out % bsz) * steps + r_out // bsz).astype(BF16)
    return pl.pallas_call(
        functools.partial(_ssm_kernel, bsz=bsz, steps=steps),
        grid=(seq // steps,),
        in_specs=[
            pl.BlockSpec((None, rows, WIDTH), lambda c: (c, 0, 0)),
            pl.BlockSpec((rows, rows), full),
            pl.BlockSpec((rows, rows), full),
            pl.BlockSpec((SSM_BLOCKS, SSM_BLOCK_IN, 2 * SSM_SLAB), lambda c: (0, 0, 0)),
            pl.BlockSpec((1, SSM_LANES), full),
            pl.BlockSpec((1, SSM_LANES), full),
            pl.BlockSpec((SSM_BLOCKS, 2 * SSM_SLAB, SSM_BLOCK_IN), lambda c: (0, 0, 0)),
            pl.BlockSpec((1, WIDTH), full),
            pl.BlockSpec((None, WIDTH, WIDTH), lambda c: (layer, 0, 0)),
            pl.BlockSpec((1, WIDTH), full),
        ],
        out_specs=pl.BlockSpec((None, rows, WIDTH), lambda c: (c, 0, 0)),
        out_shape=jax.ShapeDtypeStruct((seq // steps, rows, WIDTH), BF16),
        scratch_shapes=[pltpu.VMEM((bsz, 2 * SSM_LANES), F32),
                        pltpu.VMEM((rows, 2 * SSM_LANES), F32)],
        compiler_params=pltpu.CompilerParams(
            dimension_semantics=("arbitrary",), vmem_limit_bytes=VMEM_LIMIT),
        name="s5_scan",
    )(du_rows, perm, perm.T, bblk, ar, ai, cblk, d_skip, w_glu, b_glu)


def _ssm_params(a_re, a_im, log_dt, b_re, b_im, c_re, c_im):
    dt = jnp.exp(log_dt)[:, None]
    mag = jnp.exp(dt * a_re)
    ab_re = mag * jnp.cos(dt * a_im)
    ab_im = mag * jnp.sin(dt * a_im)
    den = a_re * a_re + a_im * a_im
    nr, ni = ab_re - 1.0, ab_im
    coef_re = (nr * a_re + ni * a_im) / den
    coef_im = (ni * a_re - nr * a_im) / den
    bb_re = coef_re[..., None] * b_re - coef_im[..., None] * b_im
    bb_im = coef_re[..., None] * b_im + coef_im[..., None] * b_re
    per = SSM_GROUPS // SSM_BLOCKS
    eye = jnp.eye(per, dtype=F32)[None, :, None, :, None]

    def diag(m):
        a, b = m.shape[1:]
        mt = m.reshape(SSM_BLOCKS, per, a, b).transpose(0, 1, 3, 2)
        return (eye * mt[:, :, :, None, :]).reshape(SSM_BLOCKS, per * b, per * a)

    bblk = jnp.concatenate([diag(bb_re), diag(bb_im)], axis=2).astype(BF16)
    cblk = jnp.concatenate([diag(c_re), -diag(c_im)], axis=1).astype(BF16)
    return bblk, ab_re.reshape(1, SSM_LANES), ab_im.reshape(1, SSM_LANES), cblk


def _merge_kernel(gate_ref, ab_ref, ac_ref, ax_ref, hc_ref, hx_ref, bu_ref, bv_ref,
                  yc_ref, yd_ref, x_ref, cw_ref, cb_ref, sgw_ref, sgb_ref, lng_ref, lnb_ref,
                  wbr_ref, wo_ref, o_ref, *, tm):
    s_idx = pl.program_id(1)

    z0 = ac_ref[...].astype(F32) * ax_ref[...].astype(F32)
    halo = hc_ref[...].astype(F32) * hx_ref[...].astype(F32)
    halo = jnp.where(s_idx == 0, jnp.zeros_like(halo), halo)
    hrows = halo.shape[0]
    h1 = halo[hrows - 1:hrows, :]
    h2 = halo[hrows - 2:hrows - 1, :]
    row = lax.broadcasted_iota(jnp.int32, z0.shape, 0)
    z1 = jnp.where(row == 0, h1, pltpu.roll(z0, 1, 0))
    z2 = jnp.where(row == 0, h2, jnp.where(row == 1, h1, pltpu.roll(z0, 2, 0)))
    conv = cw_ref[0:1, :] * z2 + cw_ref[1:2, :] * z1 + cw_ref[2:3, :] * z0 + cb_ref[...]
    y_a = ab_ref[...].astype(F32) * conv

    u = jax.nn.gelu(bu_ref[...].astype(F32))
    v = jax.nn.gelu(bv_ref[...].astype(F32))
    mu = jnp.mean(v, axis=-1, keepdims=True)
    vc = v - mu
    var = jnp.mean(vc * vc, axis=-1, keepdims=True)
    vn = (vc * lax.rsqrt(var + 1e-5) * lng_ref[...] + lnb_ref[...]).astype(BF16)
    tr = lax.broadcasted_iota(jnp.int32, (SG_CHUNK, SG_CHUNK), 0)
    tc = lax.broadcasted_iota(jnp.int32, (SG_CHUNK, SG_CHUNK), 1)
    gd = WIDTH // SG_GROUPS
    cols = []
    for g in range(SG_GROUPS):
        wg = jnp.where(tc <= tr, sgw_ref[g], 0.0).astype(BF16)
        bias = sgb_ref[:, g:g + 1]
        blocks = [jnp.dot(wg, vn[r * SG_CHUNK:(r + 1) * SG_CHUNK, g * gd:(g + 1) * gd],
                          preferred_element_type=F32) + bias
                  for r in range(tm // SG_CHUNK)]
        cols.append(jnp.concatenate(blocks, axis=0))
    y_b = u * jnp.concatenate(cols, axis=1)

    y_c = jnp.concatenate([yc_ref[hd] for hd in range(ATT_HEADS)], axis=1)
    y_d = yd_ref[...].reshape(tm, WIDTH)
    branches = (y_a.astype(BF16), y_b.astype(BF16), y_c, y_d)
    merged = None
    for n, y in enumerate(branches):
        gate = jax.nn.sigmoid(gate_ref[:, n * D_MODEL:(n + 1) * D_MODEL].astype(F32))
        term = gate * jnp.dot(y, wbr_ref[n], preferred_element_type=F32)
        merged = term if merged is None else merged + term
    o_ref[...] = x_ref[...] + jnp.dot(merged.astype(BF16), wo_ref[...],
                                      preferred_element_type=F32)


def _merge(proj3, y_c, y_d, x3, conv_w, conv_b, sg_w, sg_bt, ln_g, ln_b, w_br, w_o, layer, tm):
    bsz, seq, _ = x3.shape
    hrows = 16
    hstep = tm // hrows

    def col(blk):
        return pl.BlockSpec((None, tm, WIDTH), lambda b, s: (b, s, blk))

    def halo(blk):
        return pl.BlockSpec((None, hrows, WIDTH),
                            lambda b, s: (b, jnp.maximum(s * hstep - 1, 0), blk))

    def full(shape):
        return pl.BlockSpec(shape, lambda b, s: (0,) * len(shape))

    return pl.pallas_call(
        functools.partial(_merge_kernel, tm=tm),
        grid=(bsz, seq // tm),
        in_specs=[
            pl.BlockSpec((None, tm, GATE_COLS), lambda b, s: (b, s, 0)),
            col(BLK_AB), col(BLK_AC), col(BLK_AX), halo(BLK_AC), halo(BLK_AX),
            col(BLK_BU), col(BLK_BV),
            pl.BlockSpec((None, ATT_HEADS, tm, ATT_V_DIM), lambda b, s: (b, 0, s, 0)),
            pl.BlockSpec((tm // SSM_STEPS, SSM_STEPS, WIDTH), lambda b, s: (s, b, 0)),
            pl.BlockSpec((None, tm, D_MODEL), lambda b, s: (b, s, 0)),
            full((3, WIDTH)), full((1, WIDTH)),
            full((SG_GROUPS, SG_CHUNK, SG_CHUNK)), full((SG_CHUNK, SG_GROUPS)),
            full((1, WIDTH)), full((1, WIDTH)),
            pl.BlockSpec((None, N_BRANCH, WIDTH, D_MODEL), lambda b, s: (layer, 0, 0, 0)),
            pl.BlockSpec((None, D_MODEL, D_MODEL), lambda b, s: (layer, 0, 0)),
        ],
        out_specs=pl.BlockSpec((None, tm, D_MODEL), lambda b, s: (b, s, 0)),
        out_shape=jax.ShapeDtypeStruct((bsz, seq, D_MODEL), F32),
        compiler_params=pltpu.CompilerParams(
            dimension_semantics=("parallel", "arbitrary"), vmem_limit_bytes=VMEM_LIMIT),
        name="merge",
    )(proj3, proj3, proj3, proj3, proj3, proj3, proj3, proj3, y_c, y_d, x3,
      conv_w, conv_b, sg_w, sg_bt, ln_g, ln_b, w_br, w_o)


FFN_SPLITS = (0, 1536, FFN_HIDDEN)


def _ffn_kernel(x_ref, g_ref, wg_ref, wu_ref, wd_ref, gf_ref, o_ref, *, final):
    x = x_ref[...]
    h = _rms(x, g_ref[...], 1e-6).astype(BF16)
    acc = x
    for lo, hi in zip(FFN_SPLITS[:-1], FFN_SPLITS[1:]):
        cs = slice(lo, hi)
        gate = jnp.dot(h, wg_ref[:, cs], preferred_element_type=F32)
        up = jnp.dot(h, wu_ref[:, cs], preferred_element_type=F32)
        act = (jax.nn.silu(gate) * up).astype(BF16)
        acc = acc + jnp.dot(act, wd_ref[cs, :], preferred_element_type=F32)
    if final:
        acc = _rms(acc, gf_ref[...], 1e-6)
    o_ref[...] = acc


def _ffn(x2d, g, w_gate, w_up, w_down, g_final, layer, final, tm):
    n = x2d.shape[0]
    full = lambda i: (0, 0)
    lay = lambda i: (layer, 0, 0)
    return pl.pallas_call(
        functools.partial(_ffn_kernel, final=final),
        grid=(n // tm,),
        in_specs=[
            pl.BlockSpec((tm, D_MODEL), lambda i: (i, 0)),
            pl.BlockSpec((1, D_MODEL), full),
            pl.BlockSpec((None, D_MODEL, FFN_HIDDEN), lay),
            pl.BlockSpec((None, D_MODEL, FFN_HIDDEN), lay),
            pl.BlockSpec((None, FFN_HIDDEN, D_MODEL), lay),
            pl.BlockSpec((1, D_MODEL), full),
        ],
        out_specs=pl.BlockSpec((tm, D_MODEL), lambda i: (i, 0)),
        out_shape=jax.ShapeDtypeStruct((n, D_MODEL), F32),
        compiler_params=pltpu.CompilerParams(
            dimension_semantics=("parallel",), vmem_limit_bytes=VMEM_LIMIT),
        name="ffn",
    )(x2d, g, w_gate, w_up, w_down, g_final)


def kernel(x, g_mix, w_in, conv_w, conv_b, sg_w, sg_b, sg_ln_g, sg_ln_b, lam_qk, subln_g,
           ssm_a_re, ssm_a_im, ssm_log_dt, ssm_b_re, ssm_b_im, ssm_c_re, ssm_c_im, ssm_d,
           w_glu, b_glu, w_br, w_o, g_ffn, w_ffn_gate, w_ffn_up, w_ffn_down, g_final):
    bsz, seq, _ = x.shape
    depth = w_in.shape[0]
    n = bsz * seq
    tm = min(512, seq)
    tq = min(512, seq)
    steps = SSM_STEPS
    row2 = lambda a: a.reshape(1, -1)
    w_in_b, w_glu_b, w_br_b, w_o_b = (w.astype(BF16) for w in (w_in, w_glu, w_br, w_o))
    w_gate_b, w_up_b, w_down_b = (w.astype(BF16) for w in (w_ffn_gate, w_ffn_up, w_ffn_down))

    for l in range(depth):
        lam_init = 0.8 - 0.6 * math.exp(-0.3 * l)
        proj, qkv, du = _inproj(x.reshape(n, D_MODEL), row2(g_mix[l]), w_in_b, l, bsz, seq, tm)
        proj3 = proj.reshape(bsz, seq, PROJ_COLS)

        y_c = _attention(qkv, lam_qk[l], subln_g[l].reshape(-1, 1), lam_init, tq)

        bblk, ar, ai, cblk = _ssm_params(ssm_a_re[l], ssm_a_im[l], ssm_log_dt[l],
                                         ssm_b_re[l], ssm_b_im[l], ssm_c_re[l], ssm_c_im[l])
        y_d = _ssm(du, bblk, ar, ai, cblk, row2(ssm_d[l]),
                   w_glu_b, row2(b_glu[l]), l, bsz, seq, steps)

        x = _merge(proj3, y_c, y_d, x, conv_w[l], row2(conv_b[l]), sg_w[l], sg_b[l].T,
                   row2(sg_ln_g[l]), row2(sg_ln_b[l]), w_br_b, w_o_b, l, tm)

        x = _ffn(x.reshape(n, D_MODEL), row2(g_ffn[l]), w_gate_b, w_up_b, w_down_b,
                 row2(g_final), l, l == depth - 1, tm).reshape(bsz, seq, D_MODEL)
    return x
```

```python
import functools
import math

import jax
import jax.numpy as jnp
from jax import lax
from jax.experimental import pallas as pl
from jax.experimental.pallas import tpu as pltpu

F32 = jnp.float32
BF16 = jnp.bfloat16

D_MODEL = 1024
WIDTH = 512
N_BRANCH = 4
CHUNK = 64
SG_CHUNK = 128
SG_GROUPS = 4
ATT_HEADS = 4
ATT_QK_DIM = 64
ATT_V_DIM = 128
SSM_GROUP = 16
SSM_GROUPS = 32
SSM_STATE = 64
SSM_LANES = SSM_GROUPS * SSM_STATE
FFN_HIDDEN = 2816

GATE_COLS = N_BRANCH * D_MODEL
GATE_BLOCKS = GATE_COLS // WIDTH
BLK_AB, BLK_AC, BLK_AX, BLK_BU, BLK_BV, BLK_Q, BLK_K, BLK_V, BLK_DU = range(8, 17)
PROJ_BLOCKS = 13
PROJ_COLS = PROJ_BLOCKS * WIDTH
IN_BLOCKS = 17
SSM_STEPS = 64

VMEM_LIMIT = 56 * 1024 * 1024


def _rms(x, g, eps):
    return x * lax.rsqrt(jnp.mean(x * x, axis=-1, keepdims=True) + eps) * g


def _inproj_kernel(x_ref, g_ref, w_ref, o_ref, qkv_ref, du_ref):
    h = _rms(x_ref[...], g_ref[...], 1e-6).astype(BF16)
    for j in range(IN_BLOCKS):
        cols = slice(j * WIDTH, (j + 1) * WIDTH)
        src = (j + IN_BLOCKS - GATE_BLOCKS) % IN_BLOCKS * WIDTH
        acc = jnp.dot(h, w_ref[:, src:src + WIDTH], preferred_element_type=F32).astype(BF16)
        if j == BLK_DU:
            du_ref[...] = acc.reshape(du_ref.shape)
        elif j in (BLK_Q, BLK_K, BLK_V):
            for hd in range(ATT_HEADS):
                qkv_ref[j - BLK_Q, hd] = acc[:, hd * ATT_V_DIM:(hd + 1) * ATT_V_DIM]
        else:
            o_ref[:, cols] = acc


def _inproj(x2d, g, w, layer, bsz, seq, tm):
    n = x2d.shape[0]
    spb = seq // tm
    return pl.pallas_call(
        _inproj_kernel,
        grid=(n // tm,),
        in_specs=[
            pl.BlockSpec((tm, D_MODEL), lambda i: (i, 0)),
            pl.BlockSpec((1, D_MODEL), lambda i: (0, 0)),
            pl.BlockSpec((None, D_MODEL, IN_BLOCKS * WIDTH), lambda i: (layer, 0, 0),
                         pipeline_mode=pl.Buffered(1)),
        ],
        out_specs=[
            pl.BlockSpec((tm, PROJ_COLS), lambda i: (i, 0)),
            pl.BlockSpec((3, None, ATT_HEADS, tm, ATT_V_DIM),
                         lambda i: (0, i // spb, 0, i % spb, 0)),
            pl.BlockSpec((tm // SSM_STEPS, SSM_STEPS, WIDTH), lambda i: (i % spb, i // spb, 0)),
        ],
        out_shape=[
            jax.ShapeDtypeStruct((n, PROJ_COLS), BF16),
            jax.ShapeDtypeStruct((3, bsz, ATT_HEADS, seq, ATT_V_DIM), BF16),
            jax.ShapeDtypeStruct((seq // SSM_STEPS, bsz * SSM_STEPS, WIDTH), BF16),
        ],
        compiler_params=pltpu.CompilerParams(
            dimension_semantics=("parallel",), vmem_limit_bytes=VMEM_LIMIT),
        name="inproj",
    )(x2d, g, w)


ATT_ROWS = 16


def _attn_kernel(lam_ref, g_ref, q_ref, k_ref, v_ref, o_ref, vt_ref, qh_ref, s_ref, p_ref,
                 acc_ref, *, tq, lam_init):
    nq = q_ref.shape[0] // tq
    nchunk = tq // ATT_ROWS

    for c in range(nq):
        vt_ref[c] = v_ref[c * tq:(c + 1) * tq, :].T

    lf = lam_ref[...]
    lam = (jnp.exp(jnp.sum(lf[0:1] * lf[1:2], keepdims=True))
           - jnp.exp(jnp.sum(lf[2:3] * lf[3:4], keepdims=True)) + lam_init)

    def rows(i):
        return pl.ds(pl.multiple_of(i * tq, tq), tq)

    feat = lax.broadcasted_iota(jnp.int32, (ATT_V_DIM, tq), 0)

    def set_q(h, qi):
        qt = (q_ref[rows(qi), :] * (ATT_QK_DIM ** -0.5)).T
        keep = feat < ATT_QK_DIM if h == 0 else feat >= ATT_QK_DIM
        qh_ref[h] = jnp.where(keep, qt, jnp.zeros_like(qt))

    def scores(h, kb):
        s_ref[h] = jnp.dot(k_ref[rows(kb), :], qh_ref[h], preferred_element_type=F32)

    qlane = lax.broadcasted_iota(jnp.int32, (ATT_ROWS, tq), 1)

    def chunk(h, c, masked):
        x = s_ref[h, c * ATT_ROWS:(c + 1) * ATT_ROWS, :]
        if masked:
            x = jnp.where(qlane >= (c * ATT_ROWS) // CHUNK * CHUNK, x, -jnp.inf)
        return x

    def softmax(h, stats, masked):
        m, l = stats
        mx = chunk(h, 0, masked)
        for c in range(1, nchunk):
            mx = jnp.maximum(mx, chunk(h, c, masked))
        m_new = jnp.maximum(m, jnp.max(mx, axis=0, keepdims=True))
        alpha = jnp.exp(m - m_new)
        mb = jnp.broadcast_to(m_new, (ATT_ROWS, tq))
        tot = None
        for c in range(nchunk):
            p = jnp.exp(chunk(h, c, masked) - mb)
            tot = p if tot is None else tot + p
            p_ref[h, c * ATT_ROWS:(c + 1) * ATT_ROWS, :] = p.astype(BF16)
        l = alpha * l + jnp.sum(tot, axis=0, keepdims=True)
        return alpha, (m_new, l)

    def values(h, kb, alpha):
        acc_ref[h] = alpha * acc_ref[h] + jnp.dot(vt_ref[kb], p_ref[h],
                                                  preferred_element_type=F32)

    def body(kb, carry):
        st0, st1, alpha1 = carry
        values(1, jnp.maximum(kb - 1, 0), alpha1)
        scores(1, kb)
        alpha0, st0 = softmax(0, st0, False)
        values(0, kb, alpha0)
        scores(0, kb + 1)
        alpha1, st1 = softmax(1, st1, False)
        return st0, st1, alpha1

    def tile(qi, _):
        set_q(1, qi)
        p_ref[1] = jnp.zeros(p_ref.shape[1:], BF16)
        acc_ref[...] = jnp.zeros_like(acc_ref)
        stat0 = (jnp.full((1, tq), -jnp.inf, F32), jnp.zeros((1, tq), F32))
        st0, st1, alpha1 = lax.fori_loop(0, qi, body, (stat0, stat0, jnp.ones((1, tq), F32)))

        values(1, jnp.maximum(qi - 1, 0), alpha1)
        scores(1, qi)
        alpha0, (_, l0) = softmax(0, st0, True)
        values(0, qi, alpha0)
        set_q(0, jnp.minimum(qi + 1, nq - 1))
        scores(0, 0)
        alpha1, (_, l1) = softmax(1, st1, True)
        values(1, qi, alpha1)

        o = acc_ref[0] * (1.0 / l0) - acc_ref[1] * (lam / l1)
        scale = lax.rsqrt(jnp.mean(o * o, axis=0, keepdims=True) + 1e-5) * (1.0 - lam_init)
        o_ref[rows(qi), :] = (o * scale * g_ref[...]).T.astype(o_ref.dtype)
        return 0

    set_q(0, 0)
    scores(0, 0)
    lax.fori_loop(0, nq, tile, 0)


def _attention(qkv, lam_qk, subln_g, lam_init, tq):
    _, bsz, _, seq, lanes = qkv.shape
    return pl.pallas_call(
        functools.partial(_attn_kernel, tq=tq, lam_init=lam_init),
        grid=(bsz, ATT_HEADS),
        in_specs=[
            pl.BlockSpec((4, ATT_QK_DIM), lambda b, h: (0, 0)),
            pl.BlockSpec((ATT_V_DIM, 1), lambda b, h: (0, 0)),
            pl.BlockSpec((None, None, None, seq, lanes), lambda b, h: (0, b, h, 0, 0)),
            pl.BlockSpec((None, None, None, seq, lanes), lambda b, h: (1, b, h, 0, 0)),
            pl.BlockSpec((None, None, None, seq, lanes), lambda b, h: (2, b, h, 0, 0)),
        ],
        out_specs=pl.BlockSpec((None, None, seq, lanes), lambda b, h: (b, h, 0, 0)),
        out_shape=jax.ShapeDtypeStruct((bsz, ATT_HEADS, seq, lanes), BF16),
        scratch_shapes=[
            pltpu.VMEM((seq // tq, ATT_V_DIM, tq), BF16),
            pltpu.VMEM((2, lanes, tq), BF16),
            pltpu.VMEM((2, tq, tq), F32),
            pltpu.VMEM((2, tq, tq), BF16),
            pltpu.VMEM((2, ATT_V_DIM, tq), F32),
        ],
        compiler_params=pltpu.CompilerParams(
            dimension_semantics=("parallel", "parallel"), vmem_limit_bytes=VMEM_LIMIT),
        name="diff_attention",
    )(lam_qk, subln_g, qkv, qkv, qkv)


SSM_BLOCKS = 4
SSM_BLOCK_IN = WIDTH // SSM_BLOCKS
SSM_SLAB = SSM_LANES // SSM_BLOCKS


def _ssm_kernel(u_ref, perm_ref, permt_ref, bblk_ref, ar_ref, ai_ref, cblk_ref, d_ref,
                wglu_ref, bglu_ref, o_ref, st_ref, bu_ref, *, bsz, steps):
    @pl.when(pl.program_id(0) == 0)
    def _():
        st_ref[...] = jnp.zeros_like(st_ref)

    slab = SSM_SLAB
    perm = perm_ref[...]
    u = jnp.dot(perm, u_ref[...], preferred_element_type=F32).astype(BF16)

    def drive(j):
        bu_ref[:, 2 * slab * j:2 * slab * (j + 1)] = jnp.dot(
            u[:, SSM_BLOCK_IN * j:SSM_BLOCK_IN * (j + 1)], bblk_ref[j],
            preferred_element_type=F32)

    def scan(j):
        re = slice(2 * slab * j, 2 * slab * j + slab)
        im = slice(2 * slab * j + slab, 2 * slab * (j + 1))
        ar = jnp.broadcast_to(ar_ref[:, slab * j:slab * (j + 1)], (bsz, slab))
        ai = jnp.broadcast_to(ai_ref[:, slab * j:slab * (j + 1)], (bsz, slab))
        xr, xi = st_ref[:, re], st_ref[:, im]
        for t in range(steps):
            rows = slice(t * bsz, (t + 1) * bsz)
            xr, xi = (ar * xr - ai * xi + bu_ref[rows, re], ar * xi + ai * xr + bu_ref[rows, im])
            bu_ref[rows, re] = xr
            bu_ref[rows, im] = xi
        st_ref[:, re] = xr
        st_ref[:, im] = xi

    def readout(j):
        return jnp.dot(bu_ref[:, 2 * slab * j:2 * slab * (j + 1)].astype(BF16), cblk_ref[j],
                       preferred_element_type=F32)

    ys = [None] * SSM_BLOCKS
    drive(0)
    for j in range(SSM_BLOCKS):
        if j + 1 < SSM_BLOCKS:
            drive(j + 1)
        scan(j)
        if j > 0:
            ys[j - 1] = readout(j - 1)
    ys[SSM_BLOCKS - 1] = readout(SSM_BLOCKS - 1)
    y = jnp.concatenate(ys, axis=1)
    y = jax.nn.gelu(y + d_ref[...] * u.astype(F32))
    z = jnp.dot(y.astype(BF16), wglu_ref[...], preferred_element_type=F32) + bglu_ref[...]
    out = (y * jax.nn.sigmoid(z)).astype(BF16)
    o_ref[...] = jnp.dot(permt_ref[...], out, preferred_element_type=F32).astype(o_ref.dtype)


def _ssm(du_rows, bblk, ar, ai, cblk, d_skip, w_glu, b_glu, layer, bsz, seq, steps):
    rows = steps * bsz
    full = lambda c: (0, 0)
    r_out = jnp.arange(rows)[:, None]
    r_in = jnp.arange(rows)[None, :]
    perm = (r_in == (r_out % bsz) * steps + r_out // bsz).astype(BF16)
    return pl.pallas_call(
        functools.partial(_ssm_kernel, bsz=bsz, steps=steps),
        grid=(seq // steps,),
        in_specs=[
            pl.BlockSpec((None, rows, WIDTH), lambda c: (c, 0, 0)),
            pl.BlockSpec((rows, rows), full),
            pl.BlockSpec((rows, rows), full),
            pl.BlockSpec((SSM_BLOCKS, SSM_BLOCK_IN, 2 * SSM_SLAB), lambda c: (0, 0, 0)),
            pl.BlockSpec((1, SSM_LANES), full),
            pl.BlockSpec((1, SSM_LANES), full),
            pl.BlockSpec((SSM_BLOCKS, 2 * SSM_SLAB, SSM_BLOCK_IN), lambda c: (0, 0, 0)),
            pl.BlockSpec((1, WIDTH), full),
            pl.BlockSpec((None, WIDTH, WIDTH), lambda c: (layer, 0, 0)),
            pl.BlockSpec((1, WIDTH), full),
        ],
        out_specs=pl.BlockSpec((None, rows, WIDTH), lambda c: (c, 0, 0)),
        out_shape=jax.ShapeDtypeStruct((seq // steps, rows, WIDTH), BF16),
        scratch_shapes=[pltpu.VMEM((bsz, 2 * SSM_LANES), F32),
                        pltpu.VMEM((rows, 2 * SSM_LANES), F32)],
        compiler_params=pltpu.CompilerParams(
            dimension_semantics=("arbitrary",), vmem_limit_bytes=VMEM_LIMIT),
        name="s5_scan",
    )(du_rows, perm, perm.T, bblk, ar, ai, cblk, d_skip, w_glu, b_glu)


def _ssm_params(a_re, a_im, log_dt, b_re, b_im, c_re, c_im):
    dt = jnp.exp(log_dt)[:, None]
    mag = jnp.exp(dt * a_re)
    ab_re = mag * jnp.cos(dt * a_im)
    ab_im = mag * jnp.sin(dt * a_im)
    den = a_re * a_re + a_im * a_im
    nr, ni = ab_re - 1.0, ab_im
    coef_re = (nr * a_re + ni * a_im) / den
    coef_im = (ni * a_re - nr * a_im) / den
    bb_re = coef_re[..., None] * b_re - coef_im[..., None] * b_im
    bb_im = coef_re[..., None] * b_im + coef_im[..., None] * b_re
    per = SSM_GROUPS // SSM_BLOCKS
    eye = jnp.eye(per, dtype=F32)[None, :, None, :, None]

    def diag(m):
        a, b = m.shape[1:]
        mt = m.reshape(SSM_BLOCKS, per, a, b).transpose(0, 1, 3, 2)
        return (eye * mt[:, :, :, None, :]).reshape(SSM_BLOCKS, per * b, per * a)

    bblk = jnp.concatenate([diag(bb_re), diag(bb_im)], axis=2).astype(BF16)
    cblk = jnp.concatenate([diag(c_re), -diag(c_im)], axis=1).astype(BF16)
    return bblk, ab_re.reshape(1, SSM_LANES), ab_im.reshape(1, SSM_LANES), cblk


def _merge_kernel(gate_ref, ab_ref, ac_ref, ax_ref, hc_ref, hx_ref, bu_ref, bv_ref,
                  yc_ref, yd_ref, x_ref, cw_ref, cb_ref, sgw_ref, sgb_ref, lng_ref, lnb_ref,
                  wbr_ref, wo_ref, o_ref, *, tm):
    s_idx = pl.program_id(1)

    z0 = ac_ref[...].astype(F32) * ax_ref[...].astype(F32)
    halo = hc_ref[...].astype(F32) * hx_ref[...].astype(F32)
    halo = jnp.where(s_idx == 0, jnp.zeros_like(halo), halo)
    hrows = halo.shape[0]
    h1 = halo[hrows - 1:hrows, :]
    h2 = halo[hrows - 2:hrows - 1, :]
    row = lax.broadcasted_iota(jnp.int32, z0.shape, 0)
    z1 = jnp.where(row == 0, h1, pltpu.roll(z0, 1, 0))
    z2 = jnp.where(row == 0, h2, jnp.where(row == 1, h1, pltpu.roll(z0, 2, 0)))
    conv = cw_ref[0:1, :] * z2 + cw_ref[1:2, :] * z1 + cw_ref[2:3, :] * z0 + cb_ref[...]
    y_a = ab_ref[...].astype(F32) * conv

    u = jax.nn.gelu(bu_ref[...].astype(F32))
    v = jax.nn.gelu(bv_ref[...].astype(F32))
    mu = jnp.mean(v, axis=-1, keepdims=True)
    vc = v - mu
    var = jnp.mean(vc * vc, axis=-1, keepdims=True)
    vn = (vc * lax.rsqrt(var + 1e-5) * lng_ref[...] + lnb_ref[...]).astype(BF16)
    tr = lax.broadcasted_iota(jnp.int32, (SG_CHUNK, SG_CHUNK), 0)
    tc = lax.broadcasted_iota(jnp.int32, (SG_CHUNK, SG_CHUNK), 1)
    gd = WIDTH // SG_GROUPS
    cols = []
    for g in range(SG_GROUPS):
        wg = jnp.where(tc <= tr, sgw_ref[g], 0.0).astype(BF16)
        bias = sgb_ref[:, g:g + 1]
        blocks = [jnp.dot(wg, vn[r * SG_CHUNK:(r + 1) * SG_CHUNK, g * gd:(g + 1) * gd],
                          preferred_element_type=F32) + bias
                  for r in range(tm // SG_CHUNK)]
        cols.append(jnp.concatenate(blocks, axis=0))
    y_b = u * jnp.concatenate(cols, axis=1)

    y_c = jnp.concatenate([yc_ref[hd] for hd in range(ATT_HEADS)], axis=1)
    y_d = yd_ref[...].reshape(tm, WIDTH)
    branches = (y_a.astype(BF16), y_b.astype(BF16), y_c, y_d)
    merged = None
    for n, y in enumerate(branches):
        gate = jax.nn.sigmoid(gate_ref[:, n * D_MODEL:(n + 1) * D_MODEL].astype(F32))
        term = gate * jnp.dot(y, wbr_ref[n], preferred_element_type=F32)
        merged = term if merged is None else merged + term
    o_ref[...] = x_ref[...] + jnp.dot(merged.astype(BF16), wo_ref[...],
                                      preferred_element_type=F32)


def _merge(proj3, y_c, y_d, x3, conv_w, conv_b, sg_w, sg_bt, ln_g, ln_b, w_br, w_o, layer, tm):
    bsz, seq, _ = x3.shape
    hrows = 16
    hstep = tm // hrows

    def col(blk):
        return pl.BlockSpec((None, tm, WIDTH), lambda b, s: (b, s, blk))

    def halo(blk):
        return pl.BlockSpec((None, hrows, WIDTH),
                            lambda b, s: (b, jnp.maximum(s * hstep - 1, 0), blk))

    def full(shape):
        return pl.BlockSpec(shape, lambda b, s: (0,) * len(shape))

    return pl.pallas_call(
        functools.partial(_merge_kernel, tm=tm),
        grid=(bsz, seq // tm),
        in_specs=[
            pl.BlockSpec((None, tm, GATE_COLS), lambda b, s: (b, s, 0)),
            col(BLK_AB), col(BLK_AC), col(BLK_AX), halo(BLK_AC), halo(BLK_AX),
            col(BLK_BU), col(BLK_BV),
            pl.BlockSpec((None, ATT_HEADS, tm, ATT_V_DIM), lambda b, s: (b, 0, s, 0)),
            pl.BlockSpec((tm // SSM_STEPS, SSM_STEPS, WIDTH), lambda b, s: (s, b, 0)),
            pl.BlockSpec((None, tm, D_MODEL), lambda b, s: (b, s, 0)),
            full((3, WIDTH)), full((1, WIDTH)),
            full((SG_GROUPS, SG_CHUNK, SG_CHUNK)), full((SG_CHUNK, SG_GROUPS)),
            full((1, WIDTH)), full((1, WIDTH)),
            pl.BlockSpec((None, N_BRANCH, WIDTH, D_MODEL), lambda b, s: (layer, 0, 0, 0)),
            pl.BlockSpec((None, D_MODEL, D_MODEL), lambda b, s: (layer, 0, 0)),
        ],
        out_specs=pl.BlockSpec((None, tm, D_MODEL), lambda b, s: (b, s, 0)),
        out_shape=jax.ShapeDtypeStruct((bsz, seq, D_MODEL), F32),
        compiler_params=pltpu.CompilerParams(
            dimension_semantics=("parallel", "arbitrary"), vmem_limit_bytes=VMEM_LIMIT),
        name="merge",
    )(proj3, proj3, proj3, proj3, proj3, proj3, proj3, proj3, y_c, y_d, x3,
      conv_w, conv_b, sg_w, sg_bt, ln_g, ln_b, w_br, w_o)


FFN_SPLITS = (0, 1536, FFN_HIDDEN)


def _ffn_kernel(x_ref, g_ref, wg_ref, wu_ref, wd_ref, gf_ref, o_ref, *, final):
    x = x_ref[...]
    h = _rms(x, g_ref[...], 1e-6).astype(BF16)
    acc = x
    for lo, hi in zip(FFN_SPLITS[:-1], FFN_SPLITS[1:]):
        cs = slice(lo, hi)
        gate = jnp.dot(h, wg_ref[:, cs], preferred_element_type=F32)
        up = jnp.dot(h, wu_ref[:, cs], preferred_element_type=F32)
        act = (jax.nn.silu(gate) * up).astype(BF16)
        acc = acc + jnp.dot(act, wd_ref[cs, :], preferred_element_type=F32)
    if final:
        acc = _rms(acc, gf_ref[...], 1e-6)
    o_ref[...] = acc


def _ffn(x2d, g, w_gate, w_up, w_down, g_final, layer, final, tm):
    n = x2d.shape[0]
    full = lambda i: (0, 0)
    lay = lambda i: (layer, 0, 0)
    return pl.pallas_call(
        functools.partial(_ffn_kernel, final=final),
        grid=(n // tm,),
        in_specs=[
            pl.BlockSpec((tm, D_MODEL), lambda i: (i, 0)),
            pl.BlockSpec((1, D_MODEL), full),
            pl.BlockSpec((None, D_MODEL, FFN_HIDDEN), lay),
            pl.BlockSpec((None, D_MODEL, FFN_HIDDEN), lay),
            pl.BlockSpec((None, FFN_HIDDEN, D_MODEL), lay),
            pl.BlockSpec((1, D_MODEL), full),
        ],
        out_specs=pl.BlockSpec((tm, D_MODEL), lambda i: (i, 0)),
        out_shape=jax.ShapeDtypeStruct((n, D_MODEL), F32),
        compiler_params=pltpu.CompilerParams(
            dimension_semantics=("parallel",), vmem_limit_bytes=VMEM_LIMIT),
        name="ffn",
    )(x2d, g, w_gate, w_up, w_down, g_final)


def kernel(x, g_mix, w_in, conv_w, conv_b, sg_w, sg_b, sg_ln_g, sg_ln_b, lam_qk, subln_g,
           ssm_a_re, ssm_a_im, ssm_log_dt, ssm_b_re, ssm_b_im, ssm_c_re, ssm_c_im, ssm_d,
           w_glu, b_glu, w_br, w_o, g_ffn, w_ffn_gate, w_ffn_up, w_ffn_down, g_final):
    bsz, seq, _ = x.shape
    depth = w_in.shape[0]
    n = bsz * seq
    tm = min(512, seq)
    tq = min(512, seq)
    steps = SSM_STEPS
    row2 = lambda a: a.reshape(1, -1)
    w_in_b, w_glu_b, w_br_b, w_o_b = (w.astype(BF16) for w in (w_in, w_glu, w_br, w_o))
    w_gate_b, w_up_b, w_down_b = (w.astype(BF16) for w in (w_ffn_gate, w_ffn_up, w_ffn_down))

    for l in range(depth):
        lam_init = 0.8 - 0.6 * math.exp(-0.3 * l)
        proj, qkv, du = _inproj(x.reshape(n, D_MODEL), row2(g_mix[l]), w_in_b, l, bsz, seq, tm)
        proj3 = proj.reshape(bsz, seq, PROJ_COLS)

        y_c = _attention(qkv, lam_qk[l], subln_g[l].reshape(-1, 1), lam_init, tq)

        bblk, ar, ai, cblk = _ssm_params(ssm_a_re[l], ssm_a_im[l], ssm_log_dt[l],
                                         ssm_b_re[l], ssm_b_im[l], ssm_c_re[l], ssm_c_im[l])
        y_d = _ssm(du, bblk, ar, ai, cblk, row2(ssm_d[l]),
                   w_glu_b, row2(b_glu[l]), l, bsz, seq, steps)

        x = _merge(proj3, y_c, y_d, x, conv_w[l], row2(conv_b[l]), sg_w[l], sg_b[l].T,
                   row2(sg_ln_g[l]), row2(sg_ln_b[l]), w_br_b, w_o_b, l, tm)

        x = _ffn(x.reshape(n, D_MODEL), row2(g_ffn[l]), w_gate_b, w_up_b, w_down_b,
                 row2(g_final), l, l == depth - 1, tm).reshape(bsz, seq, D_MODEL)
    return x
```

```python
import functools
import math

import jax
import jax.numpy as jnp
from jax import lax
from jax.experimental import pallas as pl
from jax.experimental.pallas import tpu as pltpu

F32 = jnp.float32
BF16 = jnp.bfloat16

D_MODEL = 1024
WIDTH = 512
N_BRANCH = 4
CHUNK = 64
SG_CHUNK = 128
SG_GROUPS = 4
ATT_HEADS = 4
ATT_QK_DIM = 64
ATT_V_DIM = 128
SSM_GROUP = 16
SSM_GROUPS = 32
SSM_STATE = 64
SSM_LANES = SSM_GROUPS * SSM_STATE
FFN_HIDDEN = 2816

GATE_COLS = N_BRANCH * D_MODEL
GATE_BLOCKS = GATE_COLS // WIDTH
BLK_AB, BLK_AC, BLK_AX, BLK_BU, BLK_BV, BLK_Q, BLK_K, BLK_V, BLK_DU = range(8, 17)
PROJ_BLOCKS = 13
PROJ_COLS = PROJ_BLOCKS * WIDTH
IN_BLOCKS = 17
SSM_STEPS = 64

VMEM_LIMIT = 56 * 1024 * 1024


def _rms(x, g, eps):
    return x * lax.rsqrt(jnp.mean(x * x, axis=-1, keepdims=True) + eps) * g


def _inproj_kernel(x_ref, g_ref, w_ref, o_ref, qkv_ref, du_ref):
    h = _rms(x_ref[...], g_ref[...], 1e-6).astype(BF16)
    for j in range(IN_BLOCKS):
        cols = slice(j * WIDTH, (j + 1) * WIDTH)
        src = (j + IN_BLOCKS - GATE_BLOCKS) % IN_BLOCKS * WIDTH
        acc = jnp.dot(h, w_ref[:, src:src + WIDTH], preferred_element_type=F32).astype(BF16)
        if j == BLK_DU:
            du_ref[...] = acc.reshape(du_ref.shape)
        elif j in (BLK_Q, BLK_K, BLK_V):
            for hd in range(ATT_HEADS):
                qkv_ref[j - BLK_Q, hd] = acc[:, hd * ATT_V_DIM:(hd + 1) * ATT_V_DIM]
        else:
            o_ref[:, cols] = acc


def _inproj(x2d, g, w, layer, bsz, seq, tm):
    n = x2d.shape[0]
    spb = seq // tm
    return pl.pallas_call(
        _inproj_kernel,
        grid=(n // tm,),
        in_specs=[
            pl.BlockSpec((tm, D_MODEL), lambda i: (i, 0)),
            pl.BlockSpec((1, D_MODEL), lambda i: (0, 0)),
            pl.BlockSpec((None, D_MODEL, IN_BLOCKS * WIDTH), lambda i: (layer, 0, 0),
                         pipeline_mode=pl.Buffered(1)),
        ],
        out_specs=[
            pl.BlockSpec((tm, PROJ_COLS), lambda i: (i, 0)),
            pl.BlockSpec((3, None, ATT_HEADS, tm, ATT_V_DIM),
                         lambda i: (0, i // spb, 0, i % spb, 0)),
            pl.BlockSpec((tm // SSM_STEPS, SSM_STEPS, WIDTH), lambda i: (i % spb, i // spb, 0)),
        ],
        out_shape=[
            jax.ShapeDtypeStruct((n, PROJ_COLS), BF16),
            jax.ShapeDtypeStruct((3, bsz, ATT_HEADS, seq, ATT_V_DIM), BF16),
            jax.ShapeDtypeStruct((seq // SSM_STEPS, bsz * SSM_STEPS, WIDTH), BF16),
        ],
        compiler_params=pltpu.CompilerParams(
            dimension_semantics=("parallel",), vmem_limit_bytes=VMEM_LIMIT),
        name="inproj",
    )(x2d, g, w)


ATT_ROWS = 16


ATT_PAIR = 2
ATT_STREAMS = 2 * ATT_PAIR


def _attn_kernel(lam_ref, g_ref, q_ref, k_ref, v_ref, o_ref, vt_ref, qh_ref, s_ref, p_ref,
                 acc_ref, *, tq, lam_init):
    nq = q_ref.shape[1] // tq
    nchunk = tq // ATT_ROWS
    last = ATT_STREAMS - 1

    for hd in range(ATT_PAIR):
        for c in range(nq):
            vt_ref[hd, c] = v_ref[hd, c * tq:(c + 1) * tq, :].T

    lf = lam_ref[...]
    lam = (jnp.exp(jnp.sum(lf[0:1] * lf[1:2], keepdims=True))
           - jnp.exp(jnp.sum(lf[2:3] * lf[3:4], keepdims=True)) + lam_init)

    def rows(i):
        return pl.ds(pl.multiple_of(i * tq, tq), tq)

    feat = lax.broadcasted_iota(jnp.int32, (ATT_V_DIM, tq), 0)

    def set_q(st, qi):
        qt = (q_ref[st // 2, rows(qi), :].astype(F32)
              * (ATT_QK_DIM ** -0.5 * math.log2(math.e))).astype(BF16).T
        keep = feat < ATT_QK_DIM if st % 2 == 0 else feat >= ATT_QK_DIM
        qh_ref[st] = jnp.where(keep, qt, jnp.zeros_like(qt))

    def scores(st, kb):
        s_ref[st] = jnp.dot(k_ref[st // 2, rows(kb), :], qh_ref[st], preferred_element_type=F32)

    qlane = lax.broadcasted_iota(jnp.int32, (ATT_ROWS, tq), 1)

    def chunk(st, c, masked):
        x = s_ref[st, c * ATT_ROWS:(c + 1) * ATT_ROWS, :]
        if masked:
            x = jnp.where(qlane >= (c * ATT_ROWS) // CHUNK * CHUNK, x, -jnp.inf)
        return x

    def softmax(st, stats, masked):
        m, l = stats
        mx = chunk(st, 0, masked)
        for c in range(1, nchunk):
            mx = jnp.maximum(mx, chunk(st, c, masked))
        m_new = jnp.maximum(m, jnp.max(mx, axis=0, keepdims=True))
        alpha = jnp.exp2(m - m_new)
        mb = jnp.broadcast_to(m_new, (ATT_ROWS, tq))
        tot = None
        for c in range(nchunk):
            p = jnp.exp2(chunk(st, c, masked) - mb)
            tot = p if tot is None else tot + p
            p_ref[st, c * ATT_ROWS:(c + 1) * ATT_ROWS, :] = p.astype(BF16)
        l = alpha * l + jnp.sum(tot, axis=0, keepdims=True)
        return alpha, (m_new, l)

    def values(st, kb, alpha):
        acc_ref[st] = alpha * acc_ref[st] + jnp.dot(vt_ref[st // 2, kb], p_ref[st],
                                                    preferred_element_type=F32)

    def sweep(kb, stats, alpha_last, masked, ahead):
        stats = list(stats)
        values(last, jnp.maximum(kb - 1, 0), alpha_last)
        scores(last, kb)
        for st in range(last):
            alpha, stats[st] = softmax(st, stats[st], masked)
            values(st, kb, alpha)
            ahead(st)
        alpha_last, stats[last] = softmax(last, stats[last], masked)
        return tuple(stats), alpha_last

    def body(kb, carry):
        return sweep(kb, *carry, False, lambda st: scores(st, kb + 1))

    def tile(qi, _):
        set_q(last, qi)
        p_ref[last] = jnp.zeros(p_ref.shape[1:], BF16)
        acc_ref[...] = jnp.zeros_like(acc_ref)
        stat0 = (jnp.full((1, tq), -jnp.inf, F32), jnp.zeros((1, tq), F32))
        carry = lax.fori_loop(0, qi, body, ((stat0,) * ATT_STREAMS, jnp.ones((1, tq), F32)))

        nxt = jnp.minimum(qi + 1, nq - 1)

        def ahead(st):
            set_q(st, nxt)
            scores(st, 0)

        stats, alpha_last = sweep(qi, *carry, True, ahead)
        values(last, qi, alpha_last)

        for hd in range(ATT_PAIR):
            (_, l0), (_, l1) = stats[2 * hd], stats[2 * hd + 1]
            o = acc_ref[2 * hd] * (1.0 / l0) - acc_ref[2 * hd + 1] * (lam / l1)
            scale = lax.rsqrt(jnp.mean(o * o, axis=0, keepdims=True) + 1e-5) * (1.0 - lam_init)
            o_ref[hd, rows(qi), :] = (o * scale * g_ref[...]).T.astype(o_ref.dtype)
        return 0

    for st in range(last):
        set_q(st, 0)
        scores(st, 0)
    lax.fori_loop(0, nq, tile, 0)


def _attention(qkv, lam_qk, subln_g, lam_init, tq):
    _, bsz, _, seq, lanes = qkv.shape
    return pl.pallas_call(
        functools.partial(_attn_kernel, tq=tq, lam_init=lam_init),
        grid=(bsz, ATT_HEADS // ATT_PAIR),
        in_specs=[
            pl.BlockSpec((4, ATT_QK_DIM), lambda b, h: (0, 0)),
            pl.BlockSpec((ATT_V_DIM, 1), lambda b, h: (0, 0)),
            pl.BlockSpec((None, None, ATT_PAIR, seq, lanes), lambda b, h: (0, b, h, 0, 0)),
            pl.BlockSpec((None, None, ATT_PAIR, seq, lanes), lambda b, h: (1, b, h, 0, 0)),
            pl.BlockSpec((None, None, ATT_PAIR, seq, lanes), lambda b, h: (2, b, h, 0, 0)),
        ],
        out_specs=pl.BlockSpec((None, ATT_PAIR, seq, lanes), lambda b, h: (b, h, 0, 0)),
        out_shape=jax.ShapeDtypeStruct((bsz, ATT_HEADS, seq, lanes), BF16),
        scratch_shapes=[
            pltpu.VMEM((ATT_PAIR, seq // tq, ATT_V_DIM, tq), BF16),
            pltpu.VMEM((ATT_STREAMS, lanes, tq), BF16),
            pltpu.VMEM((ATT_STREAMS, tq, tq), F32),
            pltpu.VMEM((ATT_STREAMS, tq, tq), BF16),
            pltpu.VMEM((ATT_STREAMS, ATT_V_DIM, tq), F32),
        ],
        compiler_params=pltpu.CompilerParams(
            dimension_semantics=("parallel", "parallel"), vmem_limit_bytes=VMEM_LIMIT),
        name="diff_attention",
    )(lam_qk, subln_g, qkv, qkv, qkv)


SSM_BLOCKS = 4
SSM_BLOCK_IN = WIDTH // SSM_BLOCKS
SSM_SLAB = SSM_LANES // SSM_BLOCKS


def _ssm_kernel(u_ref, perm_ref, permt_ref, bblk_ref, ar_ref, ai_ref, cblk_ref, d_ref,
                wglu_ref, bglu_ref, o_ref, st_ref, bu_ref, *, bsz, steps):
    @pl.when(pl.program_id(0) == 0)
    def _():
        st_ref[...] = jnp.zeros_like(st_ref)

    slab = SSM_SLAB
    perm = perm_ref[...]
    u = jnp.dot(perm, u_ref[...], preferred_element_type=F32).astype(BF16)

    def drive(j):
        bu_ref[:, 2 * slab * j:2 * slab * (j + 1)] = jnp.dot(
            u[:, SSM_BLOCK_IN * j:SSM_BLOCK_IN * (j + 1)], bblk_ref[j],
            preferred_element_type=F32)

    def scan(j):
        re = slice(2 * slab * j, 2 * slab * j + slab)
        im = slice(2 * slab * j + slab, 2 * slab * (j + 1))
        ar = jnp.broadcast_to(ar_ref[:, slab * j:slab * (j + 1)], (bsz, slab))
        ai = jnp.broadcast_to(ai_ref[:, slab * j:slab * (j + 1)], (bsz, slab))
        xr, xi = st_ref[:, re], st_ref[:, im]
        for t in range(steps):
            rows = slice(t * bsz, (t + 1) * bsz)
            xr, xi = (ar * xr - ai * xi + bu_ref[rows, re], ar * xi + ai * xr + bu_ref[rows, im])
            bu_ref[rows, re] = xr
            bu_ref[rows, im] = xi
        st_ref[:, re] = xr
        st_ref[:, im] = xi

    def readout(j):
        return jnp.dot(bu_ref[:, 2 * slab * j:2 * slab * (j + 1)].astype(BF16), cblk_ref[j],
                       preferred_element_type=F32)

    ys = [None] * SSM_BLOCKS
    drive(0)
    for j in range(SSM_BLOCKS):
        if j + 1 < SSM_BLOCKS:
            drive(j + 1)
        scan(j)
        if j > 0:
            ys[j - 1] = readout(j - 1)
    ys[SSM_BLOCKS - 1] = readout(SSM_BLOCKS - 1)
    y = jnp.concatenate(ys, axis=1)
    y = jax.nn.gelu(y + d_ref[...] * u.astype(F32))
    z = jnp.dot(y.astype(BF16), wglu_ref[...], preferred_element_type=F32) + bglu_ref[...]
    out = (y * jax.nn.sigmoid(z)).astype(BF16)
    o_ref[...] = jnp.dot(permt_ref[...], out, preferred_element_type=F32).astype(o_ref.dtype)


def _ssm(du_rows, bblk, ar, ai, cblk, d_skip, w_glu, b_glu, layer, bsz, seq, steps):
    rows = steps * bsz
    full = lambda c: (0, 0)
    r_out = jnp.arange(rows)[:, None]
    r_in = jnp.arange(rows)[None, :]
    perm = (r_in == (r_out % bsz) * steps + r_out // bsz).astype(BF16)
    return pl.pallas_call(
        functools.partial(_ssm_kernel, bsz=bsz, steps=steps),
        grid=(seq // steps,),
        in_specs=[
            pl.BlockSpec((None, rows, WIDTH), lambda c: (c, 0, 0)),
            pl.BlockSpec((rows, rows), full),
            pl.BlockSpec((rows, rows), full),
            pl.BlockSpec((SSM_BLOCKS, SSM_BLOCK_IN, 2 * SSM_SLAB), lambda c: (0, 0, 0)),
            pl.BlockSpec((1, SSM_LANES), full),
            pl.BlockSpec((1, SSM_LANES), full),
            pl.BlockSpec((SSM_BLOCKS, 2 * SSM_SLAB, SSM_BLOCK_IN), lambda c: (0, 0, 0)),
            pl.BlockSpec((1, WIDTH), full),
            pl.BlockSpec((None, WIDTH, WIDTH), lambda c: (layer, 0, 0)),
            pl.BlockSpec((1, WIDTH), full),
        ],
        out_specs=pl.BlockSpec((None, rows, WIDTH), lambda c: (c, 0, 0)),
        out_shape=jax.ShapeDtypeStruct((seq // steps, rows, WIDTH), BF16),
        scratch_shapes=[pltpu.VMEM((bsz, 2 * SSM_LANES), F32),
                        pltpu.VMEM((rows, 2 * SSM_LANES), F32)],
        compiler_params=pltpu.CompilerParams(
            dimension_semantics=("arbitrary",), vmem_limit_bytes=VMEM_LIMIT),
        name="s5_scan",
    )(du_rows, perm, perm.T, bblk, ar, ai, cblk, d_skip, w_glu, b_glu)


def _ssm_params(a_re, a_im, log_dt, b_re, b_im, c_re, c_im):
    dt = jnp.exp(log_dt)[:, None]
    mag = jnp.exp(dt * a_re)
    ab_re = mag * jnp.cos(dt * a_im)
    ab_im = mag * jnp.sin(dt * a_im)
    den = a_re * a_re + a_im * a_im
    nr, ni = ab_re - 1.0, ab_im
    coef_re = (nr * a_re + ni * a_im) / den
    coef_im = (ni * a_re - nr * a_im) / den
    bb_re = coef_re[..., None] * b_re - coef_im[..., None] * b_im
    bb_im = coef_re[..., None] * b_im + coef_im[..., None] * b_re
    per = SSM_GROUPS // SSM_BLOCKS
    eye = jnp.eye(per, dtype=F32)[None, :, None, :, None]

    def diag(m):
        a, b = m.shape[1:]
        mt = m.reshape(SSM_BLOCKS, per, a, b).transpose(0, 1, 3, 2)
        return (eye * mt[:, :, :, None, :]).reshape(SSM_BLOCKS, per * b, per * a)

    bblk = jnp.concatenate([diag(bb_re), diag(bb_im)], axis=2).astype(BF16)
    cblk = jnp.concatenate([diag(c_re), -diag(c_im)], axis=1).astype(BF16)
    return bblk, ab_re.reshape(1, SSM_LANES), ab_im.reshape(1, SSM_LANES), cblk


def _merge_kernel(gate_ref, ab_ref, ac_ref, ax_ref, hc_ref, hx_ref, bu_ref, bv_ref,
                  yc_ref, yd_ref, x_ref, cw_ref, cb_ref, sgw_ref, sgb_ref, lng_ref, lnb_ref,
                  wbr_ref, wo_ref, o_ref, *, tm):
    s_idx = pl.program_id(1)

    z0 = ac_ref[...].astype(F32) * ax_ref[...].astype(F32)
    halo = hc_ref[...].astype(F32) * hx_ref[...].astype(F32)
    halo = jnp.where(s_idx == 0, jnp.zeros_like(halo), halo)
    hrows = halo.shape[0]
    h1 = halo[hrows - 1:hrows, :]
    h2 = halo[hrows - 2:hrows - 1, :]
    row = lax.broadcasted_iota(jnp.int32, z0.shape, 0)
    z1 = jnp.where(row == 0, h1, pltpu.roll(z0, 1, 0))
    z2 = jnp.where(row == 0, h2, jnp.where(row == 1, h1, pltpu.roll(z0, 2, 0)))
    conv = cw_ref[0:1, :] * z2 + cw_ref[1:2, :] * z1 + cw_ref[2:3, :] * z0 + cb_ref[...]
    y_a = ab_ref[...].astype(F32) * conv

    u = jax.nn.gelu(bu_ref[...].astype(F32))
    v = jax.nn.gelu(bv_ref[...].astype(F32))
    mu = jnp.mean(v, axis=-1, keepdims=True)
    vc = v - mu
    var = jnp.mean(vc * vc, axis=-1, keepdims=True)
    vn = (vc * lax.rsqrt(var + 1e-5) * lng_ref[...] + lnb_ref[...]).astype(BF16)
    tr = lax.broadcasted_iota(jnp.int32, (SG_CHUNK, SG_CHUNK), 0)
    tc = lax.broadcasted_iota(jnp.int32, (SG_CHUNK, SG_CHUNK), 1)
    gd = WIDTH // SG_GROUPS
    cols = []
    for g in range(SG_GROUPS):
        wg = jnp.where(tc <= tr, sgw_ref[g], 0.0).astype(BF16)
        bias = sgb_ref[:, g:g + 1]
        blocks = [jnp.dot(wg, vn[r * SG_CHUNK:(r + 1) * SG_CHUNK, g * gd:(g + 1) * gd],
                          preferred_element_type=F32) + bias
                  for r in range(tm // SG_CHUNK)]
        cols.append(jnp.concatenate(blocks, axis=0))
    y_b = u * jnp.concatenate(cols, axis=1)

    y_c = jnp.concatenate([yc_ref[hd] for hd in range(ATT_HEADS)], axis=1)
    y_d = yd_ref[...].reshape(tm, WIDTH)
    branches = (y_a.astype(BF16), y_b.astype(BF16), y_c, y_d)
    merged = None
    for n, y in enumerate(branches):
        gate = jax.nn.sigmoid(gate_ref[:, n * D_MODEL:(n + 1) * D_MODEL].astype(F32))
        term = gate * jnp.dot(y, wbr_ref[n], preferred_element_type=F32)
        merged = term if merged is None else merged + term
    o_ref[...] = x_ref[...] + jnp.dot(merged.astype(BF16), wo_ref[...],
                                      preferred_element_type=F32)


def _merge(proj3, y_c, y_d, x3, conv_w, conv_b, sg_w, sg_bt, ln_g, ln_b, w_br, w_o, layer, tm):
    bsz, seq, _ = x3.shape
    hrows = 16
    hstep = tm // hrows

    def col(blk):
        return pl.BlockSpec((None, tm, WIDTH), lambda b, s: (b, s, blk))

    def halo(blk):
        return pl.BlockSpec((None, hrows, WIDTH),
                            lambda b, s: (b, jnp.maximum(s * hstep - 1, 0), blk))

    def full(shape):
        return pl.BlockSpec(shape, lambda b, s: (0,) * len(shape))

    return pl.pallas_call(
        functools.partial(_merge_kernel, tm=tm),
        grid=(bsz, seq // tm),
        in_specs=[
            pl.BlockSpec((None, tm, GATE_COLS), lambda b, s: (b, s, 0)),
            col(BLK_AB), col(BLK_AC), col(BLK_AX), halo(BLK_AC), halo(BLK_AX),
            col(BLK_BU), col(BLK_BV),
            pl.BlockSpec((None, ATT_HEADS, tm, ATT_V_DIM), lambda b, s: (b, 0, s, 0)),
            pl.BlockSpec((tm // SSM_STEPS, SSM_STEPS, WIDTH), lambda b, s: (s, b, 0)),
            pl.BlockSpec((None, tm, D_MODEL), lambda b, s: (b, s, 0)),
            full((3, WIDTH)), full((1, WIDTH)),
            full((SG_GROUPS, SG_CHUNK, SG_CHUNK)), full((SG_CHUNK, SG_GROUPS)),
            full((1, WIDTH)), full((1, WIDTH)),
            pl.BlockSpec((None, N_BRANCH, WIDTH, D_MODEL), lambda b, s: (layer, 0, 0, 0)),
            pl.BlockSpec((None, D_MODEL, D_MODEL), lambda b, s: (layer, 0, 0)),
        ],
        out_specs=pl.BlockSpec((None, tm, D_MODEL), lambda b, s: (b, s, 0)),
        out_shape=jax.ShapeDtypeStruct((bsz, seq, D_MODEL), F32),
        compiler_params=pltpu.CompilerParams(
            dimension_semantics=("parallel", "arbitrary"), vmem_limit_bytes=VMEM_LIMIT),
        name="merge",
    )(proj3, proj3, proj3, proj3, proj3, proj3, proj3, proj3, y_c, y_d, x3,
      conv_w, conv_b, sg_w, sg_bt, ln_g, ln_b, w_br, w_o)


FFN_SPLITS = (0, 1536, FFN_HIDDEN)


def _ffn_kernel(x_ref, g_ref, wg_ref, wu_ref, wd_ref, gf_ref, o_ref, *, final):
    x = x_ref[...]
    h = _rms(x, g_ref[...], 1e-6).astype(BF16)
    acc = x
    for lo, hi in zip(FFN_SPLITS[:-1], FFN_SPLITS[1:]):
        cs = slice(lo, hi)
        gate = jnp.dot(h, wg_ref[:, cs], preferred_element_type=F32)
        up = jnp.dot(h, wu_ref[:, cs], preferred_element_type=F32)
        act = (jax.nn.silu(gate) * up).astype(BF16)
        acc = acc + jnp.dot(act, wd_ref[cs, :], preferred_element_type=F32)
    if final:
        acc = _rms(acc, gf_ref[...], 1e-6)
    o_ref[...] = acc


def _ffn(x2d, g, w_gate, w_up, w_down, g_final, layer, final, tm):
    n = x2d.shape[0]
    full = lambda i: (0, 0)
    lay = lambda i: (layer, 0, 0)
    return pl.pallas_call(
        functools.partial(_ffn_kernel, final=final),
        grid=(n // tm,),
        in_specs=[
            pl.BlockSpec((tm, D_MODEL), lambda i: (i, 0)),
            pl.BlockSpec((1, D_MODEL), full),
            pl.BlockSpec((None, D_MODEL, FFN_HIDDEN), lay),
            pl.BlockSpec((None, D_MODEL, FFN_HIDDEN), lay),
            pl.BlockSpec((None, FFN_HIDDEN, D_MODEL), lay),
            pl.BlockSpec((1, D_MODEL), full),
        ],
        out_specs=pl.BlockSpec((tm, D_MODEL), lambda i: (i, 0)),
        out_shape=jax.ShapeDtypeStruct((n, D_MODEL), F32),
        compiler_params=pltpu.CompilerParams(
            dimension_semantics=("parallel",), vmem_limit_bytes=VMEM_LIMIT),
        name="ffn",
    )(x2d, g, w_gate, w_up, w_down, g_final)


def kernel(x, g_mix, w_in, conv_w, conv_b, sg_w, sg_b, sg_ln_g, sg_ln_b, lam_qk, subln_g,
           ssm_a_re, ssm_a_im, ssm_log_dt, ssm_b_re, ssm_b_im, ssm_c_re, ssm_c_im, ssm_d,
           w_glu, b_glu, w_br, w_o, g_ffn, w_ffn_gate, w_ffn_up, w_ffn_down, g_final):
    bsz, seq, _ = x.shape
    depth = w_in.shape[0]
    n = bsz * seq
    tm = min(512, seq)
    tq = min(512, seq)
    steps = SSM_STEPS
    row2 = lambda a: a.reshape(1, -1)
    w_in_b, w_glu_b, w_br_b, w_o_b = (w.astype(BF16) for w in (w_in, w_glu, w_br, w_o))
    w_gate_b, w_up_b, w_down_b = (w.astype(BF16) for w in (w_ffn_gate, w_ffn_up, w_ffn_down))

    for l in range(depth):
        lam_init = 0.8 - 0.6 * math.exp(-0.3 * l)
        proj, qkv, du = _inproj(x.reshape(n, D_MODEL), row2(g_mix[l]), w_in_b, l, bsz, seq, tm)
        proj3 = proj.reshape(bsz, seq, PROJ_COLS)

        y_c = _attention(qkv, lam_qk[l], subln_g[l].reshape(-1, 1), lam_init, tq)

        bblk, ar, ai, cblk = _ssm_params(ssm_a_re[l], ssm_a_im[l], ssm_log_dt[l],
                                         ssm_b_re[l], ssm_b_im[l], ssm_c_re[l], ssm_c_im[l])
        y_d = _ssm(du, bblk, ar, ai, cblk, row2(ssm_d[l]),
                   w_glu_b, row2(b_glu[l]), l, bsz, seq, steps)

        x = _merge(proj3, y_c, y_d, x, conv_w[l], row2(conv_b[l]), sg_w[l], sg_b[l].T,
                   row2(sg_ln_g[l]), row2(sg_ln_b[l]), w_br_b, w_o_b, l, tm)

        x = _ffn(x.reshape(n, D_MODEL), row2(g_ffn[l]), w_gate_b, w_up_b, w_down_b,
                 row2(g_final), l, l == depth - 1, tm).reshape(bsz, seq, D_MODEL)
    return x
```

```python
import functools
import math

import jax
import jax.numpy as jnp
from jax import lax
from jax.experimental import pallas as pl
from jax.experimental.pallas import tpu as pltpu

F32 = jnp.float32
BF16 = jnp.bfloat16

D_MODEL = 1024
WIDTH = 512
N_BRANCH = 4
CHUNK = 64
SG_CHUNK = 128
SG_GROUPS = 4
ATT_HEADS = 4
ATT_QK_DIM = 64
ATT_V_DIM = 128
SSM_GROUP = 16
SSM_GROUPS = 32
SSM_STATE = 64
SSM_LANES = SSM_GROUPS * SSM_STATE
FFN_HIDDEN = 2816

GATE_COLS = N_BRANCH * D_MODEL
GATE_BLOCKS = GATE_COLS // WIDTH
BLK_AB, BLK_AC, BLK_AX, BLK_BU, BLK_BV, BLK_Q, BLK_K, BLK_V, BLK_DU = range(8, 17)
PROJ_BLOCKS = 13
PROJ_COLS = PROJ_BLOCKS * WIDTH
IN_BLOCKS = 17
SSM_STEPS = 64

VMEM_LIMIT = 56 * 1024 * 1024


def _rms(x, g, eps):
    return x * lax.rsqrt(jnp.mean(x * x, axis=-1, keepdims=True) + eps) * g


def _inproj_kernel(x_ref, g_ref, w_ref, o_ref, qkv_ref, du_ref):
    h = _rms(x_ref[...], g_ref[...], 1e-6).astype(BF16)
    for j in range(IN_BLOCKS):
        cols = slice(j * WIDTH, (j + 1) * WIDTH)
        src = (j + IN_BLOCKS - GATE_BLOCKS) % IN_BLOCKS * WIDTH
        acc = jnp.dot(h, w_ref[:, src:src + WIDTH], preferred_element_type=F32).astype(BF16)
        if j == BLK_DU:
            du_ref[...] = acc.reshape(du_ref.shape)
        elif j in (BLK_Q, BLK_K, BLK_V):
            for hd in range(ATT_HEADS):
                qkv_ref[j - BLK_Q, hd] = acc[:, hd * ATT_V_DIM:(hd + 1) * ATT_V_DIM]
        else:
            o_ref[:, cols] = acc


def _inproj(x2d, g, w, layer, bsz, seq, tm):
    n = x2d.shape[0]
    spb = seq // tm
    return pl.pallas_call(
        _inproj_kernel,
        grid=(n // tm,),
        in_specs=[
            pl.BlockSpec((tm, D_MODEL), lambda i: (i, 0)),
            pl.BlockSpec((1, D_MODEL), lambda i: (0, 0)),
            pl.BlockSpec((None, D_MODEL, IN_BLOCKS * WIDTH), lambda i: (layer, 0, 0),
                         pipeline_mode=pl.Buffered(1)),
        ],
        out_specs=[
            pl.BlockSpec((tm, PROJ_COLS), lambda i: (i, 0)),
            pl.BlockSpec((3, None, ATT_HEADS, tm, ATT_V_DIM),
                         lambda i: (0, i // spb, 0, i % spb, 0)),
            pl.BlockSpec((tm // SSM_STEPS, SSM_STEPS, WIDTH), lambda i: (i % spb, i // spb, 0)),
        ],
        out_shape=[
            jax.ShapeDtypeStruct((n, PROJ_COLS), BF16),
            jax.ShapeDtypeStruct((3, bsz, ATT_HEADS, seq, ATT_V_DIM), BF16),
            jax.ShapeDtypeStruct((seq // SSM_STEPS, bsz * SSM_STEPS, WIDTH), BF16),
        ],
        compiler_params=pltpu.CompilerParams(
            dimension_semantics=("parallel",), vmem_limit_bytes=VMEM_LIMIT),
        name="inproj",
    )(x2d, g, w)


ATT_ROWS = 16


ATT_PAIR = 4
ATT_STREAMS = 2 * ATT_PAIR


def _attn_kernel(lam_ref, g_ref, q_ref, k_ref, v_ref, o_ref, vt_ref, qh_ref, s_ref, p_ref,
                 acc_ref, *, tq, lam_init):
    nq = q_ref.shape[1] // tq
    nchunk = tq // ATT_ROWS
    last = ATT_STREAMS - 1

    for hd in range(ATT_PAIR):
        for c in range(nq):
            vt_ref[hd, c] = v_ref[hd, c * tq:(c + 1) * tq, :].T

    lf = lam_ref[...]
    lam = (jnp.exp(jnp.sum(lf[0:1] * lf[1:2], keepdims=True))
           - jnp.exp(jnp.sum(lf[2:3] * lf[3:4], keepdims=True)) + lam_init)

    def rows(i):
        return pl.ds(pl.multiple_of(i * tq, tq), tq)

    feat = lax.broadcasted_iota(jnp.int32, (ATT_V_DIM, tq), 0)

    def set_q(st, qi):
        qt = (q_ref[st // 2, rows(qi), :].astype(F32)
              * (ATT_QK_DIM ** -0.5 * math.log2(math.e))).astype(BF16).T
        keep = feat < ATT_QK_DIM if st % 2 == 0 else feat >= ATT_QK_DIM
        qh_ref[st] = jnp.where(keep, qt, jnp.zeros_like(qt))

    def scores(st, kb):
        s_ref[st] = jnp.dot(k_ref[st // 2, rows(kb), :], qh_ref[st], preferred_element_type=F32)

    qlane = lax.broadcasted_iota(jnp.int32, (ATT_ROWS, tq), 1)

    def chunk(st, c, masked):
        x = s_ref[st, c * ATT_ROWS:(c + 1) * ATT_ROWS, :]
        if masked:
            x = jnp.where(qlane >= (c * ATT_ROWS) // CHUNK * CHUNK, x, -jnp.inf)
        return x

    def softmax(st, stats, masked):
        m, l = stats
        mx = chunk(st, 0, masked)
        for c in range(1, nchunk):
            mx = jnp.maximum(mx, chunk(st, c, masked))
        m_new = jnp.maximum(m, jnp.max(mx, axis=0, keepdims=True))
        alpha = jnp.exp2(m - m_new)
        mb = jnp.broadcast_to(m_new, (ATT_ROWS, tq))
        tot = None
        for c in range(nchunk):
            p = jnp.exp2(chunk(st, c, masked) - mb)
            tot = p if tot is None else tot + p
            p_ref[st, c * ATT_ROWS:(c + 1) * ATT_ROWS, :] = p.astype(BF16)
        l = alpha * l + jnp.sum(tot, axis=0, keepdims=True)
        return alpha, (m_new, l)

    def values(st, kb, alpha):
        acc_ref[st] = alpha * acc_ref[st] + jnp.dot(vt_ref[st // 2, kb], p_ref[st],
                                                    preferred_element_type=F32)

    def sweep(kb, stats, alpha_last, masked, ahead):
        stats = list(stats)
        values(last, jnp.maximum(kb - 1, 0), alpha_last)
        scores(last, kb)
        for st in range(last):
            alpha, stats[st] = softmax(st, stats[st], masked)
            values(st, kb, alpha)
            ahead(st)
        alpha_last, stats[last] = softmax(last, stats[last], masked)
        return tuple(stats), alpha_last

    def body(kb, carry):
        return sweep(kb, *carry, False, lambda st: scores(st, kb + 1))

    def tile(qi, _):
        set_q(last, qi)
        p_ref[last] = jnp.zeros(p_ref.shape[1:], BF16)
        acc_ref[...] = jnp.zeros_like(acc_ref)
        stat0 = (jnp.full((1, tq), -jnp.inf, F32), jnp.zeros((1, tq), F32))
        carry = lax.fori_loop(0, qi, body, ((stat0,) * ATT_STREAMS, jnp.ones((1, tq), F32)))

        nxt = jnp.minimum(qi + 1, nq - 1)

        def ahead(st):
            set_q(st, nxt)
            scores(st, 0)

        stats, alpha_last = sweep(qi, *carry, True, ahead)
        values(last, qi, alpha_last)

        for hd in range(ATT_PAIR):
            (_, l0), (_, l1) = stats[2 * hd], stats[2 * hd + 1]
            o = acc_ref[2 * hd] * (1.0 / l0) - acc_ref[2 * hd + 1] * (lam / l1)
            scale = lax.rsqrt(jnp.mean(o * o, axis=0, keepdims=True) + 1e-5) * (1.0 - lam_init)
            o_ref[hd, rows(qi), :] = (o * scale * g_ref[...]).T.astype(o_ref.dtype)
        return 0

    for st in range(last):
        set_q(st, 0)
        scores(st, 0)
    lax.fori_loop(0, nq, tile, 0)


def _attention(qkv, lam_qk, subln_g, lam_init, tq):
    _, bsz, _, seq, lanes = qkv.shape
    return pl.pallas_call(
        functools.partial(_attn_kernel, tq=tq, lam_init=lam_init),
        grid=(bsz, ATT_HEADS // ATT_PAIR),
        in_specs=[
            pl.BlockSpec((4, ATT_QK_DIM), lambda b, h: (0, 0)),
            pl.BlockSpec((ATT_V_DIM, 1), lambda b, h: (0, 0)),
            pl.BlockSpec((None, None, ATT_PAIR, seq, lanes), lambda b, h: (0, b, h, 0, 0)),
            pl.BlockSpec((None, None, ATT_PAIR, seq, lanes), lambda b, h: (1, b, h, 0, 0)),
            pl.BlockSpec((None, None, ATT_PAIR, seq, lanes), lambda b, h: (2, b, h, 0, 0)),
        ],
        out_specs=pl.BlockSpec((None, ATT_PAIR, seq, lanes), lambda b, h: (b, h, 0, 0)),
        out_shape=jax.ShapeDtypeStruct((bsz, ATT_HEADS, seq, lanes), BF16),
        scratch_shapes=[
            pltpu.VMEM((ATT_PAIR, seq // tq, ATT_V_DIM, tq), BF16),
            pltpu.VMEM((ATT_STREAMS, lanes, tq), BF16),
            pltpu.VMEM((ATT_STREAMS, tq, tq), F32),
            pltpu.VMEM((ATT_STREAMS, tq, tq), BF16),
            pltpu.VMEM((ATT_STREAMS, ATT_V_DIM, tq), F32),
        ],
        compiler_params=pltpu.CompilerParams(
            dimension_semantics=("parallel", "parallel"), vmem_limit_bytes=VMEM_LIMIT),
        name="diff_attention",
    )(lam_qk, subln_g, qkv, qkv, qkv)


SSM_BLOCKS = 4
SSM_BLOCK_IN = WIDTH // SSM_BLOCKS
SSM_SLAB = SSM_LANES // SSM_BLOCKS


def _ssm_kernel(u_ref, perm_ref, permt_ref, bblk_ref, ar_ref, ai_ref, cblk_ref, d_ref,
                wglu_ref, bglu_ref, o_ref, st_ref, bu_ref, *, bsz, steps):
    @pl.when(pl.program_id(0) == 0)
    def _():
        st_ref[...] = jnp.zeros_like(st_ref)

    slab = SSM_SLAB
    perm = perm_ref[...]
    u = jnp.dot(perm, u_ref[...], preferred_element_type=F32).astype(BF16)

    def drive(j):
        bu_ref[:, 2 * slab * j:2 * slab * (j + 1)] = jnp.dot(
            u[:, SSM_BLOCK_IN * j:SSM_BLOCK_IN * (j + 1)], bblk_ref[j],
            preferred_element_type=F32)

    def scan(j):
        re = slice(2 * slab * j, 2 * slab * j + slab)
        im = slice(2 * slab * j + slab, 2 * slab * (j + 1))
        ar = jnp.broadcast_to(ar_ref[:, slab * j:slab * (j + 1)], (bsz, slab))
        ai = jnp.broadcast_to(ai_ref[:, slab * j:slab * (j + 1)], (bsz, slab))
        xr, xi = st_ref[:, re], st_ref[:, im]
        for t in range(steps):
            rows = slice(t * bsz, (t + 1) * bsz)
            xr, xi = (ar * xr - ai * xi + bu_ref[rows, re], ar * xi + ai * xr + bu_ref[rows, im])
            bu_ref[rows, re] = xr
            bu_ref[rows, im] = xi
        st_ref[:, re] = xr
        st_ref[:, im] = xi

    def readout(j):
        return jnp.dot(bu_ref[:, 2 * slab * j:2 * slab * (j + 1)].astype(BF16), cblk_ref[j],
                       preferred_element_type=F32)

    ys = [None] * SSM_BLOCKS
    drive(0)
    for j in range(SSM_BLOCKS):
        if j + 1 < SSM_BLOCKS:
            drive(j + 1)
        scan(j)
        if j > 0:
            ys[j - 1] = readout(j - 1)
    ys[SSM_BLOCKS - 1] = readout(SSM_BLOCKS - 1)
    y = jnp.concatenate(ys, axis=1)
    y = jax.nn.gelu(y + d_ref[...] * u.astype(F32))
    z = jnp.dot(y.astype(BF16), wglu_ref[...], preferred_element_type=F32) + bglu_ref[...]
    out = (y * jax.nn.sigmoid(z)).astype(BF16)
    o_ref[...] = jnp.dot(permt_ref[...], out, preferred_element_type=F32).astype(o_ref.dtype)


def _ssm(du_rows, bblk, ar, ai, cblk, d_skip, w_glu, b_glu, layer, bsz, seq, steps):
    rows = steps * bsz
    full = lambda c: (0, 0)
    r_out = jnp.arange(rows)[:, None]
    r_in = jnp.arange(rows)[None, :]
    perm = (r_in == (r_out % bsz) * steps + r_out // bsz).astype(BF16)
    return pl.pallas_call(
        functools.partial(_ssm_kernel, bsz=bsz, steps=steps),
        grid=(seq // steps,),
        in_specs=[
            pl.BlockSpec((None, rows, WIDTH), lambda c: (c, 0, 0)),
            pl.BlockSpec((rows, rows), full),
            pl.BlockSpec((rows, rows), full),
            pl.BlockSpec((SSM_BLOCKS, SSM_BLOCK_IN, 2 * SSM_SLAB), lambda c: (0, 0, 0)),
            pl.BlockSpec((1, SSM_LANES), full),
            pl.BlockSpec((1, SSM_LANES), full),
            pl.BlockSpec((SSM_BLOCKS, 2 * SSM_SLAB, SSM_BLOCK_IN), lambda c: (0, 0, 0)),
            pl.BlockSpec((1, WIDTH), full),
            pl.BlockSpec((None, WIDTH, WIDTH), lambda c: (layer, 0, 0)),
            pl.BlockSpec((1, WIDTH), full),
        ],
        out_specs=pl.BlockSpec((None, rows, WIDTH), lambda c: (c, 0, 0)),
        out_shape=jax.ShapeDtypeStruct((seq // steps, rows, WIDTH), BF16),
        scratch_shapes=[pltpu.VMEM((bsz, 2 * SSM_LANES), F32),
                        pltpu.VMEM((rows, 2 * SSM_LANES), F32)],
        compiler_params=pltpu.CompilerParams(
            dimension_semantics=("arbitrary",), vmem_limit_bytes=VMEM_LIMIT),
        name="s5_scan",
    )(du_rows, perm, perm.T, bblk, ar, ai, cblk, d_skip, w_glu, b_glu)


def _ssm_params(a_re, a_im, log_dt, b_re, b_im, c_re, c_im):
    dt = jnp.exp(log_dt)[:, None]
    mag = jnp.exp(dt * a_re)
    ab_re = mag * jnp.cos(dt * a_im)
    ab_im = mag * jnp.sin(dt * a_im)
    den = a_re * a_re + a_im * a_im
    nr, ni = ab_re - 1.0, ab_im
    coef_re = (nr * a_re + ni * a_im) / den
    coef_im = (ni * a_re - nr * a_im) / den
    bb_re = coef_re[..., None] * b_re - coef_im[..., None] * b_im
    bb_im = coef_re[..., None] * b_im + coef_im[..., None] * b_re
    per = SSM_GROUPS // SSM_BLOCKS
    eye = jnp.eye(per, dtype=F32)[None, :, None, :, None]

    def diag(m):
        a, b = m.shape[1:]
        mt = m.reshape(SSM_BLOCKS, per, a, b).transpose(0, 1, 3, 2)
        return (eye * mt[:, :, :, None, :]).reshape(SSM_BLOCKS, per * b, per * a)

    bblk = jnp.concatenate([diag(bb_re), diag(bb_im)], axis=2).astype(BF16)
    cblk = jnp.concatenate([diag(c_re), -diag(c_im)], axis=1).astype(BF16)
    return bblk, ab_re.reshape(1, SSM_LANES), ab_im.reshape(1, SSM_LANES), cblk


def _merge_kernel(gate_ref, ab_ref, ac_ref, ax_ref, hc_ref, hx_ref, bu_ref, bv_ref,
                  yc_ref, yd_ref, x_ref, cw_ref, cb_ref, sgw_ref, sgb_ref, lng_ref, lnb_ref,
                  wbr_ref, wo_ref, o_ref, *, tm):
    s_idx = pl.program_id(1)

    z0 = ac_ref[...].astype(F32) * ax_ref[...].astype(F32)
    halo = hc_ref[...].astype(F32) * hx_ref[...].astype(F32)
    halo = jnp.where(s_idx == 0, jnp.zeros_like(halo), halo)
    hrows = halo.shape[0]
    h1 = halo[hrows - 1:hrows, :]
    h2 = halo[hrows - 2:hrows - 1, :]
    row = lax.broadcasted_iota(jnp.int32, z0.shape, 0)
    z1 = jnp.where(row == 0, h1, pltpu.roll(z0, 1, 0))
    z2 = jnp.where(row == 0, h2, jnp.where(row == 1, h1, pltpu.roll(z0, 2, 0)))
    conv = cw_ref[0:1, :] * z2 + cw_ref[1:2, :] * z1 + cw_ref[2:3, :] * z0 + cb_ref[...]
    y_a = ab_ref[...].astype(F32) * conv

    u = jax.nn.gelu(bu_ref[...].astype(F32))
    v = jax.nn.gelu(bv_ref[...].astype(F32))
    mu = jnp.mean(v, axis=-1, keepdims=True)
    vc = v - mu
    var = jnp.mean(vc * vc, axis=-1, keepdims=True)
    vn = (vc * lax.rsqrt(var + 1e-5) * lng_ref[...] + lnb_ref[...]).astype(BF16)
    tr = lax.broadcasted_iota(jnp.int32, (SG_CHUNK, SG_CHUNK), 0)
    tc = lax.broadcasted_iota(jnp.int32, (SG_CHUNK, SG_CHUNK), 1)
    gd = WIDTH // SG_GROUPS
    cols = []
    for g in range(SG_GROUPS):
        wg = jnp.where(tc <= tr, sgw_ref[g], 0.0).astype(BF16)
        bias = sgb_ref[:, g:g + 1]
        blocks = [jnp.dot(wg, vn[r * SG_CHUNK:(r + 1) * SG_CHUNK, g * gd:(g + 1) * gd],
                          preferred_element_type=F32) + bias
                  for r in range(tm // SG_CHUNK)]
        cols.append(jnp.concatenate(blocks, axis=0))
    y_b = u * jnp.concatenate(cols, axis=1)

    y_c = jnp.concatenate([yc_ref[hd] for hd in range(ATT_HEADS)], axis=1)
    y_d = yd_ref[...].reshape(tm, WIDTH)
    branches = (y_a.astype(BF16), y_b.astype(BF16), y_c, y_d)
    merged = None
    for n, y in enumerate(branches):
        gate = jax.nn.sigmoid(gate_ref[:, n * D_MODEL:(n + 1) * D_MODEL].astype(F32))
        term = gate * jnp.dot(y, wbr_ref[n], preferred_element_type=F32)
        merged = term if merged is None else merged + term
    o_ref[...] = x_ref[...] + jnp.dot(merged.astype(BF16), wo_ref[...],
                                      preferred_element_type=F32)


def _merge(proj3, y_c, y_d, x3, conv_w, conv_b, sg_w, sg_bt, ln_g, ln_b, w_br, w_o, layer, tm):
    bsz, seq, _ = x3.shape
    hrows = 16
    hstep = tm // hrows

    def col(blk):
        return pl.BlockSpec((None, tm, WIDTH), lambda b, s: (b, s, blk))

    def halo(blk):
        return pl.BlockSpec((None, hrows, WIDTH),
                            lambda b, s: (b, jnp.maximum(s * hstep - 1, 0), blk))

    def full(shape):
        return pl.BlockSpec(shape, lambda b, s: (0,) * len(shape))

    return pl.pallas_call(
        functools.partial(_merge_kernel, tm=tm),
        grid=(bsz, seq // tm),
        in_specs=[
            pl.BlockSpec((None, tm, GATE_COLS), lambda b, s: (b, s, 0)),
            col(BLK_AB), col(BLK_AC), col(BLK_AX), halo(BLK_AC), halo(BLK_AX),
            col(BLK_BU), col(BLK_BV),
            pl.BlockSpec((None, ATT_HEADS, tm, ATT_V_DIM), lambda b, s: (b, 0, s, 0)),
            pl.BlockSpec((tm // SSM_STEPS, SSM_STEPS, WIDTH), lambda b, s: (s, b, 0)),
            pl.BlockSpec((None, tm, D_MODEL), lambda b, s: (b, s, 0)),
            full((3, WIDTH)), full((1, WIDTH)),
            full((SG_GROUPS, SG_CHUNK, SG_CHUNK)), full((SG_CHUNK, SG_GROUPS)),
            full((1, WIDTH)), full((1, WIDTH)),
            pl.BlockSpec((None, N_BRANCH, WIDTH, D_MODEL), lambda b, s: (layer, 0, 0, 0)),
            pl.BlockSpec((None, D_MODEL, D_MODEL), lambda b, s: (layer, 0, 0)),
        ],
        out_specs=pl.BlockSpec((None, tm, D_MODEL), lambda b, s: (b, s, 0)),
        out_shape=jax.ShapeDtypeStruct((bsz, seq, D_MODEL), F32),
        compiler_params=pltpu.CompilerParams(
            dimension_semantics=("parallel", "arbitrary"), vmem_limit_bytes=VMEM_LIMIT),
        name="merge",
    )(proj3, proj3, proj3, proj3, proj3, proj3, proj3, proj3, y_c, y_d, x3,
      conv_w, conv_b, sg_w, sg_bt, ln_g, ln_b, w_br, w_o)


FFN_SPLITS = (0, 1536, FFN_HIDDEN)


def _ffn_kernel(x_ref, g_ref, wg_ref, wu_ref, wd_ref, gf_ref, o_ref, *, final):
    x = x_ref[...]
    h = _rms(x, g_ref[...], 1e-6).astype(BF16)
    acc = x
    for lo, hi in zip(FFN_SPLITS[:-1], FFN_SPLITS[1:]):
        cs = slice(lo, hi)
        gate = jnp.dot(h, wg_ref[:, cs], preferred_element_type=F32)
        up = jnp.dot(h, wu_ref[:, cs], preferred_element_type=F32)
        act = (jax.nn.silu(gate) * up).astype(BF16)
        acc = acc + jnp.dot(act, wd_ref[cs, :], preferred_element_type=F32)
    if final:
        acc = _rms(acc, gf_ref[...], 1e-6)
    o_ref[...] = acc


def _ffn(x2d, g, w_gate, w_up, w_down, g_final, layer, final, tm):
    n = x2d.shape[0]
    full = lambda i: (0, 0)
    lay = lambda i: (layer, 0, 0)
    return pl.pallas_call(
        functools.partial(_ffn_kernel, final=final),
        grid=(n // tm,),
        in_specs=[
            pl.BlockSpec((tm, D_MODEL), lambda i: (i, 0)),
            pl.BlockSpec((1, D_MODEL), full),
            pl.BlockSpec((None, D_MODEL, FFN_HIDDEN), lay),
            pl.BlockSpec((None, D_MODEL, FFN_HIDDEN), lay),
            pl.BlockSpec((None, FFN_HIDDEN, D_MODEL), lay),
            pl.BlockSpec((1, D_MODEL), full),
        ],
        out_specs=pl.BlockSpec((tm, D_MODEL), lambda i: (i, 0)),
        out_shape=jax.ShapeDtypeStruct((n, D_MODEL), F32),
        compiler_params=pltpu.CompilerParams(
            dimension_semantics=("parallel",), vmem_limit_bytes=VMEM_LIMIT),
        name="ffn",
    )(x2d, g, w_gate, w_up, w_down, g_final)


def kernel(x, g_mix, w_in, conv_w, conv_b, sg_w, sg_b, sg_ln_g, sg_ln_b, lam_qk, subln_g,
           ssm_a_re, ssm_a_im, ssm_log_dt, ssm_b_re, ssm_b_im, ssm_c_re, ssm_c_im, ssm_d,
           w_glu, b_glu, w_br, w_o, g_ffn, w_ffn_gate, w_ffn_up, w_ffn_down, g_final):
    bsz, seq, _ = x.shape
    depth = w_in.shape[0]
    n = bsz * seq
    tm = min(512, seq)
    tq = min(512, seq)
    steps = SSM_STEPS
    row2 = lambda a: a.reshape(1, -1)
    w_in_b, w_glu_b, w_br_b, w_o_b = (w.astype(BF16) for w in (w_in, w_glu, w_br, w_o))
    w_gate_b, w_up_b, w_down_b = (w.astype(BF16) for w in (w_ffn_gate, w_ffn_up, w_ffn_down))

    for l in range(depth):
        lam_init = 0.8 - 0.6 * math.exp(-0.3 * l)
        proj, qkv, du = _inproj(x.reshape(n, D_MODEL), row2(g_mix[l]), w_in_b, l, bsz, seq, tm)
        proj3 = proj.reshape(bsz, seq, PROJ_COLS)

        y_c = _attention(qkv, lam_qk[l], subln_g[l].reshape(-1, 1), lam_init, tq)

        bblk, ar, ai, cblk = _ssm_params(ssm_a_re[l], ssm_a_im[l], ssm_log_dt[l],
                                         ssm_b_re[l], ssm_b_im[l], ssm_c_re[l], ssm_c_im[l])
        y_d = _ssm(du, bblk, ar, ai, cblk, row2(ssm_d[l]),
                   w_glu_b, row2(b_glu[l]), l, bsz, seq, steps)

        x = _merge(proj3, y_c, y_d, x, conv_w[l], row2(conv_b[l]), sg_w[l], sg_b[l].T,
                   row2(sg_ln_g[l]), row2(sg_ln_b[l]), w_br_b, w_o_b, l, tm)

        x = _ffn(x.reshape(n, D_MODEL), row2(g_ffn[l]), w_gate_b, w_up_b, w_down_b,
                 row2(g_final), l, l == depth - 1, tm).reshape(bsz, seq, D_MODEL)
    return x
```
